```python
import math
import jax, jax.numpy as jnp
from jax import lax
import numpy as np

D_MODEL = 1024
BATCH = 8
SEQ = 4096
DEPTH = 2

D_RNN = D_MODEL
N_LRU_BLOCKS = 8
LRU_BLOCK = D_RNN // N_LRU_BLOCKS
CONV_WIDTH = 4
CONV_PAD_LEFT = 1
LRU_C = 8.0
HEAD_DIM = 64
HEADS_PER_GROUP = 4
WINDOWS = (128, 512, 2048)
DILATIONS = (1, 4, 16)
N_GROUPS = len(WINDOWS)
N_ATT_HEADS = N_GROUPS * HEADS_PER_GROUP
ATT_WIDTH = N_ATT_HEADS * HEAD_DIM
ATT_OUT = HEADS_PER_GROUP * HEAD_DIM
ROT_DIM = HEAD_DIM // 4
ROPE_THETA = 500000.0
BAND_BLOCK = 64
N_BRANCHES = 2
IN_WIDTH = 2 * D_RNN + 3 * ATT_WIDTH + N_BRANCHES * D_MODEL
D_FF = 2816
N_EXPERTS = 8
TOP_K = 2
D_EXPERT = 3584
N_DENSE = (DEPTH + 1) // 2
N_MOE = DEPTH // 2
RMS_EPS = 1e-6

kernel_name = "hybrid_rglru_dilated_attn_moe_encoder"


def rms_norm(x, g):
    xf = x.astype(jnp.float32)
    y = xf * lax.rsqrt(jnp.mean(xf * xf, axis=-1, keepdims=True) + RMS_EPS)
    return (y * g.astype(jnp.float32)).astype(x.dtype)


def rotary_tables(seq_len):
    pos = jnp.arange(seq_len, dtype=jnp.float32)
    inv_freq = ROPE_THETA ** (-jnp.arange(0, ROT_DIM, 2, dtype=jnp.float32) / ROT_DIM)
    ang = pos[:, None] * inv_freq[None, :]
    return jnp.cos(ang), jnp.sin(ang)


def partial_rotary(t, cos, sin):
    half = ROT_DIM // 2
    rot = t[..., :ROT_DIM].astype(jnp.float32)
    t1, t2 = rot[..., :half], rot[..., half:]
    c, s = cos[None, :, None, :], sin[None, :, None, :]
    rotated = jnp.concatenate([t1 * c - t2 * s, t2 * c + t1 * s], axis=-1).astype(t.dtype)
    return jnp.concatenate([rotated, t[..., ROT_DIM:]], axis=-1)


def centred_depthwise_conv(x, w, b):
    S = x.shape[1]
    xp = jnp.pad(x, ((0, 0), (CONV_PAD_LEFT, CONV_WIDTH - 1 - CONV_PAD_LEFT), (0, 0)))
    y = b
    for j in range(CONV_WIDTH):
        y = y + xp[:, j:j + S] * w[j]
    return y


def _linear_scan_combine(left, right):
    a_l, b_l = left
    a_r, b_r = right
    return a_l * a_r, a_r * b_l + b_r


def rg_lru_direction(xc, w_r, b_r, w_i, b_i, lam, reverse):
    B, S, _ = xc.shape
    xb = xc.reshape(B, S, N_LRU_BLOCKS, LRU_BLOCK)
    r = jax.nn.sigmoid(jnp.einsum('bsnc,ncd->bsnd', xb, w_r.astype(jnp.float32)).reshape(B, S, D_RNN) + b_r)
    i = jax.nn.sigmoid(jnp.einsum('bsnc,ncd->bsnd', xb, w_i.astype(jnp.float32)).reshape(B, S, D_RNN) + b_i)
    log_a = -LRU_C * r * jax.nn.softplus(-lam.astype(jnp.float32))
    a = jnp.exp(log_a)
    u = jnp.sqrt(-jnp.expm1(2.0 * log_a)) * (i * xc)
    if reverse:
        a, u = jnp.flip(a, axis=1), jnp.flip(u, axis=1)
    _, h = lax.associative_scan(_linear_scan_combine, (a, u), axis=1)
    if reverse:
        h = jnp.flip(h, axis=1)
    return h


def dilated_band_attention(q, k, v, dilation, half_keys):
    B, S, H, Dh = q.shape
    L = S // dilation

    def by_residue(t):
        return t.reshape(B, L, dilation, H, Dh).transpose(0, 2, 1, 3, 4)

    qs, ks, vs = by_residue(q), by_residue(k), by_residue(v)
    C = math.gcd(L, BAND_BLOCK)
    nb = L // C
    W = C + 2 * half_keys
    pad = ((0, 0), (0, 0), (half_keys, half_keys), (0, 0), (0, 0))
    kp, vp = jnp.pad(ks, pad), jnp.pad(vs, pad)
    idx = (jnp.arange(nb) * C)[:, None] + jnp.arange(W)[None, :]
    kb = kp[:, :, idx].astype(jnp.float32)
    vb = vp[:, :, idx].astype(jnp.float32)
    qb = qs.reshape(B, dilation, nb, C, H, Dh).astype(jnp.float32)
    s = jnp.einsum('brnqhd,brnkhd->brnhqk', qb, kb) * (HEAD_DIM ** -0.5)
    rel = jnp.arange(W)[None, :] - half_keys - jnp.arange(C)[:, None]
    band = jnp.abs(rel) <= half_keys
    kpos = idx - half_keys
    in_range = (kpos >= 0) & (kpos < L)
    mask = band[None, :, :] & in_range[:, None, :]
    s = jnp.where(mask[None, None, :, None, :, :], s, -jnp.inf)
    lse = jax.nn.logsumexp(s, axis=-1)
    p = jnp.exp(s - lse[..., None])
    o = jnp.einsum('brnhqk,brnkhd->brnqhd', p, vb)
    o = o.reshape(B, dilation, L, H, Dh).transpose(0, 2, 1, 3, 4).reshape(B, S, H, Dh)
    lse = lse.transpose(0, 1, 2, 4, 3).reshape(B, dilation, L, H).transpose(0, 2, 1, 3).reshape(B, S, H)
    return o, lse


def hybrid_mixer(h, w_in, conv_w, conv_b, w_rgate, b_rgate, w_igate, b_igate, lru_lambda,
                 w_proj_lru, w_proj_att, w_out, cos, sin):
    B, S, _ = h.shape
    z = h @ w_in
    cuts = [int(c) for c in np.cumsum([D_RNN, D_RNN, ATT_WIDTH, ATT_WIDTH, ATT_WIDTH])]
    x_rnn, y_gate, q, k, v, g = jnp.split(z, cuts, axis=-1)

    xc = centred_depthwise_conv(x_rnn, conv_w, conv_b).astype(jnp.float32)
    h_fwd = rg_lru_direction(xc, w_rgate[0], b_rgate[0], w_igate[0], b_igate[0], lru_lambda[0], False)
    h_bwd = rg_lru_direction(xc, w_rgate[1], b_rgate[1], w_igate[1], b_igate[1], lru_lambda[1], True)
    lru = ((h_fwd + h_bwd) * jax.nn.gelu(y_gate.astype(jnp.float32), approximate=True)).astype(h.dtype)
    branch_lru = lru @ w_proj_lru

    q = partial_rotary(q.reshape(B, S, N_ATT_HEADS, HEAD_DIM), cos, sin)
    k = partial_rotary(k.reshape(B, S, N_ATT_HEADS, HEAD_DIM), cos, sin)
    v = v.reshape(B, S, N_ATT_HEADS, HEAD_DIM)
    outs, lses = [], []
    for grp in range(N_GROUPS):
        sl = slice(grp * HEADS_PER_GROUP, (grp + 1) * HEADS_PER_GROUP)
        o, lse = dilated_band_attention(q[:, :, sl], k[:, :, sl], v[:, :, sl],
                                        DILATIONS[grp], WINDOWS[grp] // (2 * DILATIONS[grp]))
        outs.append(o)
        lses.append(lse)
    mix = jax.nn.softmax(jnp.stack(lses, axis=0), axis=0)
    att = jnp.sum(mix[..., None] * jnp.stack(outs, axis=0), axis=0)
    branch_att = att.reshape(B, S, ATT_OUT).astype(h.dtype) @ w_proj_att

    gates = jax.nn.sigmoid(g.astype(jnp.float32)).reshape(B, S, N_BRANCHES, D_MODEL)
    merged = gates[:, :, 0] * branch_lru.astype(jnp.float32) + gates[:, :, 1] * branch_att.astype(jnp.float32)
    return merged.astype(h.dtype) @ w_out


def swiglu(h, w_gate, w_up, w_down):
    return (jax.nn.silu(h @ w_gate) * (h @ w_up)) @ w_down


def moe_swiglu(h, w_router, w_gate, w_up, w_down):
    logits = (h @ w_router).astype(jnp.float32)
    top_val, top_idx = lax.top_k(logits, TOP_K)
    top_w = jax.nn.softmax(top_val, axis=-1)
    combine = jnp.sum(jax.nn.one_hot(top_idx, N_EXPERTS, dtype=jnp.float32) * top_w[..., None], axis=-2)
    out = jnp.zeros(h.shape, jnp.float32)
    for e in range(N_EXPERTS):
        y = swiglu(h, w_gate[e], w_up[e], w_down[e]).astype(jnp.float32)
        out = out + combine[..., e:e + 1] * y
    return out.astype(h.dtype)


def setup_inputs(seed: int = 0) -> dict:
    key = jax.random.key(seed)
    ks = jax.random.split(key, 24)
    f32 = jnp.float32

    def nrm(k, shape, fan_in):
        return jax.random.normal(k, shape, f32) * (fan_in ** -0.5)

    u = jax.random.uniform(ks[10], (DEPTH, 2, D_RNN), f32, 0.9, 0.999)
    a0 = u ** (1.0 / LRU_C)
    lru_lambda = jnp.log(a0) - jnp.log1p(-a0)
    return {
        "x": jax.random.normal(ks[0], (BATCH, SEQ, D_MODEL), f32),
        "norm_mix_g": 1.0 + 0.02 * jax.random.normal(ks[1], (DEPTH, D_MODEL), f32),
        "w_in": nrm(ks[2], (DEPTH, D_MODEL, IN_WIDTH), D_MODEL),
        "conv_w": nrm(ks[3], (DEPTH, CONV_WIDTH, D_RNN), CONV_WIDTH),
        "conv_b": 0.02 * jax.random.normal(ks[4], (DEPTH, D_RNN), f32),
        "w_rgate": nrm(ks[5], (DEPTH, 2, N_LRU_BLOCKS, LRU_BLOCK, LRU_BLOCK), LRU_BLOCK),
        "b_rgate": 0.02 * jax.random.normal(ks[6], (DEPTH, 2, D_RNN), f32),
        "w_igate": nrm(ks[7], (DEPTH, 2, N_LRU_BLOCKS, LRU_BLOCK, LRU_BLOCK), LRU_BLOCK),
        "b_igate": 0.02 * jax.random.normal(ks[8], (DEPTH, 2, D_RNN), f32),
        "lru_lambda": lru_lambda,
        "w_proj_lru": nrm(ks[11], (DEPTH, D_RNN, D_MODEL), D_RNN),
        "w_proj_att": nrm(ks[12], (DEPTH, ATT_OUT, D_MODEL), ATT_OUT),
        "w_out": nrm(ks[13], (DEPTH, D_MODEL, D_MODEL), D_MODEL),
        "norm_ffn_g": 1.0 + 0.02 * jax.random.normal(ks[14], (DEPTH, D_MODEL), f32),
        "w_dense_gate": nrm(ks[15], (N_DENSE, D_MODEL, D_FF), D_MODEL),
        "w_dense_up": nrm(ks[16], (N_DENSE, D_MODEL, D_FF), D_MODEL),
        "w_dense_down": nrm(ks[17], (N_DENSE, D_FF, D_MODEL), D_FF),
        "w_router": nrm(ks[18], (N_MOE, D_MODEL, N_EXPERTS), D_MODEL),
        "w_exp_gate": nrm(ks[19], (N_MOE, N_EXPERTS, D_MODEL, D_EXPERT), D_MODEL),
        "w_exp_up": nrm(ks[20], (N_MOE, N_EXPERTS, D_MODEL, D_EXPERT), D_MODEL),
        "w_exp_down": nrm(ks[21], (N_MOE, N_EXPERTS, D_EXPERT, D_MODEL), D_EXPERT),
        "final_norm_g": 1.0 + 0.02 * jax.random.normal(ks[22], (D_MODEL,), f32),
    }


def reference(x, norm_mix_g, w_in, conv_w, conv_b, w_rgate, b_rgate, w_igate, b_igate, lru_lambda,
              w_proj_lru, w_proj_att, w_out, norm_ffn_g, w_dense_gate, w_dense_up, w_dense_down,
              w_router, w_exp_gate, w_exp_up, w_exp_down, final_norm_g):
    cos, sin = rotary_tables(x.shape[1])
    for layer in range(DEPTH):
        hn = rms_norm(x, norm_mix_g[layer])
        x = x + hybrid_mixer(hn, w_in[layer], conv_w[layer], conv_b[layer], w_rgate[layer], b_rgate[layer],
                             w_igate[layer], b_igate[layer], lru_lambda[layer], w_proj_lru[layer],
                             w_proj_att[layer], w_out[layer], cos, sin)
        hn = rms_norm(x, norm_ffn_g[layer])
        j = layer // 2
        if layer % 2 == 0:
            x = x + swiglu(hn, w_dense_gate[j], w_dense_up[j], w_dense_down[j])
        else:
            x = x + moe_swiglu(hn, w_router[j], w_exp_gate[j], w_exp_up[j], w_exp_down[j])
    return rms_norm(x, final_norm_g)
```

```python
import functools

import jax
import jax.numpy as jnp
import numpy as np
from jax import lax
from jax.experimental import pallas as pl
from jax.experimental.pallas import tpu as pltpu

D_MODEL = 1024
D_RNN = D_MODEL
N_LRU_BLOCKS = 8
LRU_BLOCK = D_RNN // N_LRU_BLOCKS
CONV_WIDTH = 4
LRU_C = 8.0
HEAD_DIM = 64
HEADS_PER_GROUP = 4
WINDOWS = (128, 512, 2048)
DILATIONS = (1, 4, 16)
N_GROUPS = len(WINDOWS)
ATT_WIDTH = N_GROUPS * HEADS_PER_GROUP * HEAD_DIM
ATT_OUT = HEADS_PER_GROUP * HEAD_DIM
ROT_DIM = HEAD_DIM // 4
ROPE_THETA = 500000.0
HALF_KEYS = 64
IN_WIDTH = 2 * D_RNN + 3 * ATT_WIDTH + 2 * D_MODEL
QKV_WIDTH = 3 * ATT_WIDTH
YG_WIDTH = 3 * D_MODEL
N_EXPERTS = 8
TOP_K = 2
RMS_EPS = 1e-6

LANES = 128
SUBLANES = 8
VMEM_LIMIT = 52 * 1024 * 1024

TM = 512
LRU_TS = 256
LRU_CB = 256
ATT_CQ = 128
ATT_KW = ATT_CQ + 2 * HALF_KEYS
MOE_TM = 512
MOE_FC = 512
GATHER_TM = 256

BF16 = jnp.bfloat16
F32 = jnp.float32


def _cparams(sem):
    return pltpu.CompilerParams(dimension_semantics=sem, vmem_limit_bytes=VMEM_LIMIT)


def _resident(shape, index_map):
    return pl.BlockSpec(shape, index_map, pipeline_mode=pl.Buffered(1))


def _sigmoid(x):
    return 0.5 * jnp.tanh(0.5 * x) + 0.5


def _rms(x, g):
    ms = jnp.mean(x * x, axis=-1, keepdims=True)
    return x * lax.rsqrt(ms + RMS_EPS) * g


def _in_proj_kernel(x_ref, g_ref, w_ref, cos_ref, sa_ref, sb_ref, xr_ref, yg_ref, qkv_ref):
    hn = _rms(x_ref[...], g_ref[...]).astype(BF16)

    def proj(c0, width):
        return jnp.dot(hn, w_ref[:, c0:c0 + width], preferred_element_type=F32)

    for c in range(D_RNN // 512):
        xr_ref[:, c * 512:(c + 1) * 512] = proj(c * 512, 512)
    for c in range(YG_WIDTH // 512):
        yg_ref[:, c * 512:(c + 1) * 512] = proj(D_RNN + c * 512, 512).astype(BF16)

    cosf, sina, sinb = cos_ref[...], sa_ref[...], sb_ref[...]
    base = D_RNN + YG_WIDTH
    for c in range(QKV_WIDTH // 256):
        acc = proj(base + c * 256, 256)
        if c < 2 * ATT_WIDTH // 256:
            scale = HEAD_DIM ** -0.5 if c < ATT_WIDTH // 256 else 1.0
            for h in range(2):
                t = acc[:, h * LANES:(h + 1) * LANES]
                r = t * cosf + pltpu.roll(t, 8, 1) * sina + pltpu.roll(t, LANES - 8, 1) * sinb
                qkv_ref[:, c * 256 + h * LANES:c * 256 + (h + 1) * LANES] = (r * scale).astype(BF16)
        else:
            qkv_ref[:, c * 256:(c + 1) * 256] = acc.astype(BF16)


def _in_proj(x2, g, w_perm, cosf, sina, sinb, batch, seq):
    tokens = batch * seq
    ns = seq // TM
    return pl.pallas_call(
        _in_proj_kernel,
        grid=(tokens // TM,),
        in_specs=[
            pl.BlockSpec((TM, D_MODEL), lambda i: (i, 0)),
            _resident((1, D_MODEL), lambda i: (0, 0)),
            _resident((D_MODEL, IN_WIDTH), lambda i: (0, 0)),
            pl.BlockSpec((TM, LANES), lambda i: (i % ns, 0)),
            pl.BlockSpec((TM, LANES), lambda i: (i % ns, 0)),
            pl.BlockSpec((TM, LANES), lambda i: (i % ns, 0)),
        ],
        out_specs=[
            pl.BlockSpec((TM, D_RNN), lambda i: (i % ns, i // ns)),
            pl.BlockSpec((TM, YG_WIDTH), lambda i: (i, 0)),
            pl.BlockSpec((TM, QKV_WIDTH), lambda i: (i, 0)),
        ],
        out_shape=[
            jax.ShapeDtypeStruct((seq, batch * D_RNN), F32),
            jax.ShapeDtypeStruct((tokens, YG_WIDTH), BF16),
            jax.ShapeDtypeStruct((tokens, QKV_WIDTH), BF16),
        ],
        compiler_params=_cparams(("parallel",)),
        name="in_proj",
    )(x2, g, w_perm, cosf, sina, sinb)


def _lru_kernel(xf_ref, xfp_ref, xfn_ref, xb_ref, xbp_ref, xbn_ref, cw_ref, cb_ref, wg_ref,
                br_ref, bi_ref, lam_ref, hf_ref, hb_ref, xp_scr, a_scr, u_scr, carry_scr):
    c = pl.program_id(1)
    nc = pl.num_programs(1)
    ts = xf_ref.shape[0]
    nb = xf_ref.shape[1]

    @pl.when(c == 0)
    def _():
        carry_scr[...] = jnp.zeros_like(carry_scr)

    def gates(direction, x_ref, prev_ref, next_ref, chunk):
        prev = jnp.where(chunk > 0, prev_ref[SUBLANES - 1], 0.0)
        nxt0 = jnp.where(chunk < nc - 1, next_ref[0], 0.0)
        nxt1 = jnp.where(chunk < nc - 1, next_ref[1], 0.0)
        xp_scr[0] = prev
        xp_scr[1:ts + 1] = x_ref[...]
        xp_scr[ts + 1] = nxt0
        xp_scr[ts + 2] = nxt1
        cw = cw_ref[...]
        xc = cb_ref[...][None]
        for j in range(CONV_WIDTH):
            xc = xc + xp_scr[j:j + ts] * cw[j:j + 1][None]
        for j in range(LRU_CB // LRU_BLOCK):
            sl = slice(j * LRU_BLOCK, (j + 1) * LRU_BLOCK)
            xcb = xc[:, :, sl].reshape(ts * nb, LRU_BLOCK)
            rg = jnp.dot(xcb.astype(BF16), wg_ref[direction, j], preferred_element_type=F32)
            r = _sigmoid(rg[:, :LRU_BLOCK] + br_ref[direction:direction + 1, sl])
            ig = _sigmoid(rg[:, LRU_BLOCK:] + bi_ref[direction:direction + 1, sl])
            lam = lam_ref[direction:direction + 1, sl]
            kneg = -LRU_C * (jnp.maximum(-lam, 0.0) + jnp.log(1.0 + jnp.exp(-jnp.abs(lam))))
            a = jnp.exp(r * kneg)
            u = jnp.sqrt(1.0 - a * a) * (ig * xcb)
            a_scr[direction, :, :, sl] = a.reshape(ts, nb, LRU_BLOCK)
            u_scr[direction, :, :, sl] = u.reshape(ts, nb, LRU_BLOCK)

    gates(0, xf_ref, xfp_ref, xfn_ref, c)
    gates(1, xb_ref, xbp_ref, xbn_ref, nc - 1 - c)

    def step(t, carry):
        hf, hb = carry
        tb = ts - 1 - t
        hf = a_scr[0, t] * hf + u_scr[0, t]
        hb = a_scr[1, tb] * hb + u_scr[1, tb]
        hf_ref[t] = hf
        hb_ref[tb] = hb
        return hf, hb

    hf, hb = lax.fori_loop(0, ts, step, (carry_scr[0], carry_scr[1]), unroll=8)
    carry_scr[0] = hf
    carry_scr[1] = hb


def _lru(xr3, conv_w, conv_b, wgate, b_r, b_i, lam):
    seq, batch, _ = xr3.shape
    nc = seq // LRU_TS
    ncb = D_RNN // LRU_CB
    hblk = LRU_TS // SUBLANES
    nh = seq // SUBLANES

    def cur(rev):
        return lambda n, c: ((nc - 1 - c) if rev else c, 0, n)

    def prev(rev):
        return lambda n, c: (jnp.maximum(((nc - 1 - c) if rev else c) * hblk - 1, 0), 0, n)

    def nxt(rev):
        return lambda n, c: (jnp.minimum((((nc - 1 - c) if rev else c) + 1) * hblk, nh - 1), 0, n)

    main = lambda rev: pl.BlockSpec((LRU_TS, batch, LRU_CB), cur(rev))
    halo_p = lambda rev: pl.BlockSpec((SUBLANES, batch, LRU_CB), prev(rev))
    halo_n = lambda rev: pl.BlockSpec((SUBLANES, batch, LRU_CB), nxt(rev))
    vec2 = pl.BlockSpec((2, LRU_CB), lambda n, c: (0, n))
    out_sds = jax.ShapeDtypeStruct((seq, batch, D_RNN), F32)
    return pl.pallas_call(
        _lru_kernel,
        grid=(ncb, nc),
        in_specs=[
            main(False), halo_p(False), halo_n(False),
            main(True), halo_p(True), halo_n(True),
            pl.BlockSpec((CONV_WIDTH, LRU_CB), lambda n, c: (0, n)),
            pl.BlockSpec((1, LRU_CB), lambda n, c: (0, n)),
            pl.BlockSpec((2, LRU_CB // LRU_BLOCK, LRU_BLOCK, 2 * LRU_BLOCK), lambda n, c: (0, n, 0, 0)),
            vec2, vec2, vec2,
        ],
        out_specs=[main(False), main(True)],
        out_shape=[out_sds, out_sds],
        scratch_shapes=[
            pltpu.VMEM((LRU_TS + CONV_WIDTH - 1, batch, LRU_CB), F32),
            pltpu.VMEM((2, LRU_TS, batch, LRU_CB), F32),
            pltpu.VMEM((2, LRU_TS, batch, LRU_CB), F32),
            pltpu.VMEM((2, batch, LRU_CB), F32),
        ],
        compiler_params=_cparams(("parallel", "arbitrary")),
        name="lru",
    )(xr3, xr3, xr3, xr3, xr3, xr3, conv_w, conv_b, wgate, b_r, b_i, lam)


def _attn_kernel(q_ref, k_ref, v_ref, o_ref, lse_ref):
    length = q_ref.shape[0]
    lane_head = lax.broadcasted_iota(jnp.int32, (1, ATT_OUT), 1) // HEAD_DIM
    row = lax.broadcasted_iota(jnp.int32, (ATT_CQ, ATT_KW), 0)
    col = lax.broadcasted_iota(jnp.int32, (ATT_CQ, ATT_KW), 1)

    def chunk(ci, _):
        qs = pl.multiple_of(ci * ATT_CQ, ATT_CQ)
        ks = pl.multiple_of(jnp.clip(qs - HALF_KEYS, 0, length - ATT_KW), HALF_KEYS)
        q = q_ref[pl.ds(qs, ATT_CQ), :]
        kk = k_ref[pl.ds(ks, ATT_KW), :]
        vv = v_ref[pl.ds(ks, ATT_KW), :]
        band = jnp.abs((col + ks) - (row + qs)) <= HALF_KEYS
        o = jnp.zeros((ATT_CQ, ATT_OUT), F32)
        lse = jnp.zeros((ATT_CQ, ATT_OUT), F32)
        for h in range(HEADS_PER_GROUP):
            sel = lane_head == h
            qh = jnp.where(sel, q, jnp.zeros_like(q))
            s = lax.dot_general(qh, kk, (((1,), (1,)), ((), ())), preferred_element_type=F32)
            s = jnp.where(band, s, -jnp.inf)
            m = jnp.max(s, axis=1, keepdims=True)
            p = jnp.exp(s - m)
            l = jnp.sum(p, axis=1, keepdims=True)
            pv = jnp.dot(p.astype(BF16), vv, preferred_element_type=F32)
            o = jnp.where(sel, pv * (1.0 / l), o)
            lse = jnp.where(sel, m + jnp.log(l), lse)
        o_ref[pl.ds(qs, ATT_CQ), :] = o.astype(o_ref.dtype)
        lse_ref[pl.ds(qs, ATT_CQ), :] = lse
        return 0

    lax.fori_loop(0, length // ATT_CQ, chunk, 0)


def _attention(qkv, group, batch, seq):
    d = DILATIONS[group]
    length = seq // d
    blocks_per_res = QKV_WIDTH // ATT_OUT
    qkv3 = qkv.reshape(batch, length, d * QKV_WIDTH)
    nq = ATT_WIDTH // ATT_OUT

    def spec(which):
        return pl.BlockSpec((None, length, ATT_OUT),
                            lambda b, r: (b, 0, r * blocks_per_res + which * nq + group))

    out_spec = pl.BlockSpec((None, length, ATT_OUT), lambda b, r: (b, 0, r))
    o, lse = pl.pallas_call(
        _attn_kernel,
        grid=(batch, d),
        in_specs=[spec(0), spec(1), spec(2)],
        out_specs=[out_spec, out_spec],
        out_shape=[jax.ShapeDtypeStruct((batch, length, d * ATT_OUT), BF16),
                   jax.ShapeDtypeStruct((batch, length, d * ATT_OUT), F32)],
        compiler_params=_cparams(("parallel", "parallel")),
        name=f"attention_d{d}",
    )(qkv3, qkv3, qkv3)
    return o.reshape(batch * seq, ATT_OUT), lse.reshape(batch * seq, ATT_OUT)


def _merge_kernel(x_ref, hf_ref, hb_ref, yg_ref, o0_ref, o1_ref, o2_ref, l0_ref, l1_ref, l2_ref,
                  wpl_ref, wpa_ref, wo_ref, out_ref):
    y = yg_ref[:, :D_MODEL].astype(F32)
    gelu = 0.5 * y * (1.0 + jnp.tanh(np.sqrt(2.0 / np.pi) * (y + 0.044715 * (y * y * y))))
    lru = ((hf_ref[...] + hb_ref[...]) * gelu).astype(BF16)
    branch_lru = jnp.dot(lru, wpl_ref[...], preferred_element_type=F32)

    l0, l1, l2 = l0_ref[...], l1_ref[...], l2_ref[...]
    m = jnp.maximum(jnp.maximum(l0, l1), l2)
    e0, e1, e2 = jnp.exp(l0 - m), jnp.exp(l1 - m), jnp.exp(l2 - m)
    num = e0 * o0_ref[...].astype(F32) + e1 * o1_ref[...].astype(F32) + e2 * o2_ref[...].astype(F32)
    att = (num * (1.0 / (e0 + e1 + e2))).astype(BF16)
    branch_att = jnp.dot(att, wpa_ref[...], preferred_element_type=F32)

    g_lru = _sigmoid(yg_ref[:, D_MODEL:2 * D_MODEL].astype(F32))
    g_att = _sigmoid(yg_ref[:, 2 * D_MODEL:].astype(F32))
    merged = (g_lru * branch_lru + g_att * branch_att).astype(BF16)
    out_ref[...] = x_ref[...] + jnp.dot(merged, wo_ref[...], preferred_element_type=F32)


def _merge(x2, hf2, hb2, yg, outs, lses, wpl, wpa, wo, batch, seq):
    tokens = batch * seq
    ns = seq // TM
    tok = lambda w: pl.BlockSpec((TM, w), lambda i: (i, 0))
    tmaj = pl.BlockSpec((TM, D_RNN), lambda i: (i % ns, i // ns))
    return pl.pallas_call(
        _merge_kernel,
        grid=(tokens // TM,),
        in_specs=[tok(D_MODEL), tmaj, tmaj, tok(YG_WIDTH)] + [tok(ATT_OUT)] * 6 + [
            _resident((D_RNN, D_MODEL), lambda i: (0, 0)),
            _resident((ATT_OUT, D_MODEL), lambda i: (0, 0)),
            _resident((D_MODEL, D_MODEL), lambda i: (0, 0)),
        ],
        out_specs=tok(D_MODEL),
        out_shape=jax.ShapeDtypeStruct((tokens, D_MODEL), F32),
        compiler_params=_cparams(("parallel",)),
        name="merge",
    )(x2, hf2, hb2, yg, *outs, *lses, wpl, wpa, wo)


def _ffn_chunks(d_ff):
    tiles = d_ff // 256
    sizes = [tiles // 3 + (1 if k < tiles % 3 else 0) for k in range(3)]
    out, start = [], 0
    for s in sizes:
        if s:
            out.append((start * 256, s * 256))
            start += s
    return out


def _ffn_kernel(x_ref, g_ref, wg_ref, wu_ref, wd_ref, out_ref):
    x = x_ref[...]
    hn = _rms(x, g_ref[...]).astype(BF16)
    acc = x
    for c0, width in _ffn_chunks(wg_ref.shape[1]):
        gate = jnp.dot(hn, wg_ref[:, c0:c0 + width], preferred_element_type=F32)
        up = jnp.dot(hn, wu_ref[:, c0:c0 + width], preferred_element_type=F32)
        act = (gate * _sigmoid(gate) * up).astype(BF16)
        acc = acc + jnp.dot(act, wd_ref[c0:c0 + width, :], preferred_element_type=F32)
    out_ref[...] = acc


def _ffn(x2, g, wg, wu, wd):
    tokens = x2.shape[0]
    d_ff = wg.shape[1]
    return pl.pallas_call(
        _ffn_kernel,
        grid=(tokens // TM,),
        in_specs=[
            pl.BlockSpec((TM, D_MODEL), lambda i: (i, 0)),
            _resident((1, D_MODEL), lambda i: (0, 0)),
            _resident((D_MODEL, d_ff), lambda i: (0, 0)),
            _resident((D_MODEL, d_ff), lambda i: (0, 0)),
            _resident((d_ff, D_MODEL), lambda i: (0, 0)),
        ],
        out_specs=pl.BlockSpec((TM, D_MODEL), lambda i: (i, 0)),
        out_shape=jax.ShapeDtypeStruct((tokens, D_MODEL), F32),
        compiler_params=_cparams(("parallel",)),
        name="ffn_dense",
    )(x2, g, wg, wu, wd)


def _router_kernel(x_ref, g_ref, whi_ref, wlo_ref, tri_ref, hp_ref, wc_ref, meta_ref, cnt_ref, carry_scr):
    i = pl.program_id(0)

    @pl.when(i == 0)
    def _():
        carry_scr[...] = jnp.zeros_like(carry_scr)

    hn = _rms(x_ref[...], g_ref[...])
    hi = hn.astype(BF16)
    hi32 = hi.astype(F32)
    lo = (hn - hi32).astype(BF16)

    half = D_MODEL // 2
    bits = pltpu.bitcast(hi32, jnp.uint32)
    hp_ref[...] = (bits[:, :half] >> 16) | (bits[:, half:] & jnp.uint32(0xFFFF0000))

    whi = whi_ref[...]
    logits = (jnp.dot(hi, whi, preferred_element_type=F32) + jnp.dot(lo, whi, preferred_element_type=F32)
              + jnp.dot(hi, wlo_ref[...], preferred_element_type=F32))
    tm = logits.shape[0]
    lane = lax.broadcasted_iota(jnp.int32, (tm, LANES), 1).astype(F32)
    lg = jnp.where(lane < N_EXPERTS, logits, -jnp.inf)
    m1 = jnp.max(lg, axis=1, keepdims=True)
    i1 = jnp.min(jnp.where(lg == m1, lane, float(LANES)), axis=1, keepdims=True)
    lg2 = jnp.where(lane == i1, -jnp.inf, lg)
    m2 = jnp.max(lg2, axis=1, keepdims=True)
    i2 = jnp.min(jnp.where(lg2 == m2, lane, float(LANES)), axis=1, keepdims=True)
    e = jnp.exp(m2 - m1)
    w1 = 1.0 / (1.0 + e)
    w2 = e * w1
    wc_ref[...] = jnp.where(lane == 0.0, w1, jnp.where(lane == 1.0, w2, 0.0))

    hot1 = lane == i1
    hot2 = lane == i2
    onehot = jnp.where(hot1 | hot2, 1.0, 0.0).astype(BF16)
    cum = jnp.dot(tri_ref[...], onehot, preferred_element_type=F32)
    tot = carry_scr[...] + cum
    r1 = jnp.sum(jnp.where(hot1, tot - 1.0, 0.0), axis=1, keepdims=True)
    r2 = jnp.sum(jnp.where(hot2, tot - 1.0, 0.0), axis=1, keepdims=True)
    carry_scr[...] = tot[tm - 1:tm, :]
    cnt_ref[...] = tot[tm - 1:tm, :].astype(jnp.int32)

    cols = jnp.where(lane == 0.0, i1, jnp.where(lane == 1.0, i2, jnp.where(lane == 2.0, r1,
                     jnp.where(lane == 3.0, r2, 0.0))))
    meta_ref[...] = cols.T[:SUBLANES, :].astype(jnp.int32)


def _router(x2, g, whi, wlo):
    tokens = x2.shape[0]
    tri = jnp.tril(jnp.ones((TM, TM), F32)).astype(BF16)
    return pl.pallas_call(
        _router_kernel,
        grid=(tokens // TM,),
        in_specs=[
            pl.BlockSpec((TM, D_MODEL), lambda i: (i, 0)),
            _resident((1, D_MODEL), lambda i: (0, 0)),
            _resident((D_MODEL, LANES), lambda i: (0, 0)),
            _resident((D_MODEL, LANES), lambda i: (0, 0)),
            _resident((TM, TM), lambda i: (0, 0)),
        ],
        out_specs=[
            pl.BlockSpec((TM, D_MODEL // 2), lambda i: (i, 0)),
            pl.BlockSpec((TM, LANES), lambda i: (i, 0)),
            pl.BlockSpec((SUBLANES, TM), lambda i: (0, i)),
            pl.BlockSpec((1, LANES), lambda i: (0, 0)),
        ],
        out_shape=[
            jax.ShapeDtypeStruct((tokens, D_MODEL // 2), jnp.uint32),
            jax.ShapeDtypeStruct((tokens, LANES), F32),
            jax.ShapeDtypeStruct((SUBLANES, tokens), jnp.int32),
            jax.ShapeDtypeStruct((1, LANES), jnp.int32),
        ],
        scratch_shapes=[pltpu.VMEM((1, LANES), F32)],
        compiler_params=_cparams(("arbitrary",)),
        name="router",
    )(x2, g, whi, wlo, tri)


def _dispatch_kernel(pos_ref, hp_ref, xs_in_ref, xs_ref, sem):
    del xs_in_ref
    tm = hp_ref.shape[0]

    def copy(r, k):
        return pltpu.make_async_copy(hp_ref.at[pl.ds(r, 1)], xs_ref.at[pl.ds(pos_ref[k, r], 1)], sem)

    def start(r, _):
        copy(r, 0).start()
        copy(r, 1).start()
        return 0

    def wait(r, _):
        copy(r, 0).wait()
        copy(r, 1).wait()
        return 0

    lax.fori_loop(0, tm, start, 0)
    lax.fori_loop(0, tm, wait, 0)


def _dispatch(pos3, hp, rows):
    tokens = hp.shape[0]
    xs0 = jnp.zeros((rows, D_MODEL // 2), jnp.uint32)
    return pl.pallas_call(
        _dispatch_kernel,
        grid=(tokens // GATHER_TM,),
        in_specs=[
            pl.BlockSpec((None, TOP_K, GATHER_TM), lambda i: (i, 0, 0), memory_space=pltpu.SMEM),
            pl.BlockSpec((GATHER_TM, D_MODEL // 2), lambda i: (i, 0)),
            pl.BlockSpec(memory_space=pl.ANY),
        ],
        out_specs=pl.BlockSpec(memory_space=pl.ANY),
        out_shape=jax.ShapeDtypeStruct((rows, D_MODEL // 2), jnp.uint32),
        scratch_shapes=[pltpu.SemaphoreType.DMA],
        input_output_aliases={2: 0},
        compiler_params=_cparams(("arbitrary",)),
        name="dispatch",
    )(pos3, hp, xs0)


def _moe_kernel(te_ref, tv_ref, xs_ref, wg_ref, wu_ref, wd_ref, ys_ref):
    i = pl.program_id(0)
    half = D_MODEL // 2

    @pl.when(tv_ref[i] == 0)
    def _():
        ys_ref[...] = jnp.zeros_like(ys_ref)

    @pl.when(tv_ref[i] != 0)
    def _():
        packed = xs_ref[...]
        x_lo = pltpu.bitcast(packed << 16, F32).astype(BF16)
        x_hi = pltpu.bitcast(packed & jnp.uint32(0xFFFF0000), F32).astype(BF16)
        acc = jnp.zeros(ys_ref.shape, F32)
        for c in range(wg_ref.shape[1] // MOE_FC):
            cs = slice(c * MOE_FC, (c + 1) * MOE_FC)
            gate = (jnp.dot(x_lo, wg_ref[:half, cs], preferred_element_type=F32)
                    + jnp.dot(x_hi, wg_ref[half:, cs], preferred_element_type=F32))
            up = (jnp.dot(x_lo, wu_ref[:half, cs], preferred_element_type=F32)
                  + jnp.dot(x_hi, wu_ref[half:, cs], preferred_element_type=F32))
            act = (gate * _sigmoid(gate) * up).astype(BF16)
            acc = acc + jnp.dot(act, wd_ref[cs, :], preferred_element_type=F32)
        ys_ref[...] = acc


def _moe(tile_expert, tile_valid, xs, wg, wu, wd):
    rows = xs.shape[0]
    d_exp = wg.shape[2]
    grid_spec = pltpu.PrefetchScalarGridSpec(
        num_scalar_prefetch=2,
        grid=(rows // MOE_TM,),
        in_specs=[
            pl.BlockSpec((MOE_TM, D_MODEL // 2), lambda i, te, tv: (i, 0)),
            pl.BlockSpec((None, D_MODEL, d_exp), lambda i, te, tv: (te[i], 0, 0), pipeline_mode=pl.Buffered(1)),
            pl.BlockSpec((None, D_MODEL, d_exp), lambda i, te, tv: (te[i], 0, 0), pipeline_mode=pl.Buffered(1)),
            pl.BlockSpec((None, d_exp, D_MODEL), lambda i, te, tv: (te[i], 0, 0), pipeline_mode=pl.Buffered(1)),
        ],
        out_specs=pl.BlockSpec((MOE_TM, D_MODEL), lambda i, te, tv: (i, 0)),
    )
    return pl.pallas_call(
        _moe_kernel,
        grid_spec=grid_spec,
        out_shape=jax.ShapeDtypeStruct((rows, D_MODEL), F32),
        compiler_params=_cparams(("arbitrary",)),
        name="experts",
    )(tile_expert, tile_valid, xs, wg, wu, wd)


def _combine_kernel(pos_ref, x_ref, wc_ref, g_ref, ys_ref, out_ref, buf, sem):
    tm = x_ref.shape[0]

    def copy(r, k):
        return pltpu.make_async_copy(ys_ref.at[pl.ds(pos_ref[k, r], 1)], buf.at[k, pl.ds(r, 1)], sem)

    def start(r, _):
        copy(r, 0).start()
        copy(r, 1).start()
        return 0

    def wait(r, _):
        copy(r, 0).wait()
        copy(r, 1).wait()
        return 0

    lax.fori_loop(0, tm, start, 0)
    lax.fori_loop(0, tm, wait, 0)
    wc = wc_ref[...]
    y = x_ref[...] + (wc[:, 0:1] * buf[0] + wc[:, 1:2] * buf[1])
    out_ref[...] = _rms(y, g_ref[...])


def _combine(pos3, x2, wcols, g, ys):
    tokens = x2.shape[0]
    return pl.pallas_call(
        _combine_kernel,
        grid=(tokens // GATHER_TM,),
        in_specs=[
            pl.BlockSpec((None, TOP_K, GATHER_TM), lambda i: (i, 0, 0), memory_space=pltpu.SMEM),
            pl.BlockSpec((GATHER_TM, D_MODEL), lambda i: (i, 0)),
            pl.BlockSpec((GATHER_TM, LANES), lambda i: (i, 0)),
            _resident((1, D_MODEL), lambda i: (0, 0)),
            pl.BlockSpec(memory_space=pl.ANY),
        ],
        out_specs=pl.BlockSpec((GATHER_TM, D_MODEL), lambda i: (i, 0)),
        out_shape=jax.ShapeDtypeStruct((tokens, D_MODEL), F32),
        scratch_shapes=[pltpu.VMEM((TOP_K, GATHER_TM, D_MODEL), F32), pltpu.SemaphoreType.DMA],
        compiler_params=_cparams(("arbitrary",)),
        name="combine",
    )(pos3, x2, wcols, g, ys)


def _rotary_lane_tables(seq):
    half = ROT_DIM // 2
    pos = jnp.arange(seq, dtype=F32)
    inv_freq = ROPE_THETA ** (-jnp.arange(0, ROT_DIM, 2, dtype=F32) / ROT_DIM)
    ang = pos[:, None] * inv_freq[None, :]
    cos, sin = jnp.cos(ang), jnp.sin(ang)
    j = np.arange(LANES) % HEAD_DIM
    idx = j % half
    cosf = jnp.where((j < ROT_DIM)[None, :], cos[:, idx], 1.0)
    sina = jnp.where(((j >= half) & (j < ROT_DIM))[None, :], sin[:, idx], 0.0)
    sinb = jnp.where((j < half)[None, :], -sin[:, idx], 0.0)
    return cosf, sina, sinb


def _mixer(x2, layer, p, tables, batch, seq):
    cuts = np.cumsum([D_RNN, D_RNN, ATT_WIDTH, ATT_WIDTH, ATT_WIDTH])
    w = p["w_in"][layer]
    w_perm = jnp.concatenate([w[:, :cuts[1]], w[:, cuts[4]:], w[:, cuts[1]:cuts[4]]], axis=1).astype(BF16)
    xr, yg, qkv = _in_proj(x2, p["norm_mix_g"][layer][None], w_perm, *tables, batch, seq)

    wgate = jnp.concatenate([p["w_rgate"][layer], p["w_igate"][layer]], axis=-1).astype(BF16)
    hf, hb = _lru(xr.reshape(seq, batch, D_RNN), p["conv_w"][layer], p["conv_b"][layer][None], wgate,
                  p["b_rgate"][layer], p["b_igate"][layer], p["lru_lambda"][layer])

    outs, lses = [], []
    for grp in range(N_GROUPS):
        o, lse = _attention(qkv, grp, batch, seq)
        outs.append(o)
        lses.append(lse)

    return _merge(x2, hf.reshape(seq, batch * D_RNN), hb.reshape(seq, batch * D_RNN), yg, outs, lses,
                  p["w_proj_lru"][layer].astype(BF16), p["w_proj_att"][layer].astype(BF16),
                  p["w_out"][layer].astype(BF16), batch, seq)


def _moe_layer(x2, j, p, final_g):
    tokens = x2.shape[0]
    wr = jnp.pad(p["w_router"][j], ((0, 0), (0, LANES - N_EXPERTS)))
    whi = wr.astype(BF16)
    wlo = (wr - whi.astype(F32)).astype(BF16)
    hp, wcols, meta, counts = _router(x2, p["norm_ffn_g"][j * 2 + 1][None], whi, wlo)

    counts = counts[0, :N_EXPERTS]
    padded = ((counts + MOE_TM - 1) // MOE_TM) * MOE_TM
    ends = jnp.cumsum(padded)
    starts = ends - padded
    experts_of = meta[0:TOP_K]
    base = jnp.zeros_like(experts_of)
    for e in range(N_EXPERTS):
        base = base + jnp.where(experts_of == e, starts[e], 0)
    pos = base + meta[TOP_K:2 * TOP_K]
    pos3 = pos.reshape(TOP_K, tokens // GATHER_TM, GATHER_TM).transpose(1, 0, 2)

    rows = TOP_K * tokens + N_EXPERTS * MOE_TM
    tile_start = jnp.arange(rows // MOE_TM, dtype=jnp.int32) * MOE_TM
    tile_expert = jnp.minimum(jnp.sum(tile_start[:, None] >= ends[None, :], axis=1), N_EXPERTS - 1).astype(jnp.int32)
    tile_valid = (tile_start < ends[-1]).astype(jnp.int32)

    xs = _dispatch(pos3, hp, rows)
    ys = _moe(tile_expert, tile_valid, xs, p["w_exp_gate"][j].astype(BF16), p["w_exp_up"][j].astype(BF16),
              p["w_exp_down"][j].astype(BF16))
    return _combine(pos3, x2, wcols, final_g[None], ys)


def kernel(x, norm_mix_g, w_in, conv_w, conv_b, w_rgate, b_rgate, w_igate, b_igate, lru_lambda, w_proj_lru,
           w_proj_att, w_out, norm_ffn_g, w_dense_gate, w_dense_up, w_dense_down, w_router, w_exp_gate, w_exp_up,
           w_exp_down, final_norm_g):
    batch, seq, _ = x.shape
    depth = w_in.shape[0]
    assert depth % 2 == 0, "the final RMSNorm is fused into the last (routed) layer"
    p = dict(norm_mix_g=norm_mix_g, w_in=w_in, conv_w=conv_w, conv_b=conv_b, w_rgate=w_rgate, b_rgate=b_rgate,
             w_igate=w_igate, b_igate=b_igate, lru_lambda=lru_lambda, w_proj_lru=w_proj_lru,
             w_proj_att=w_proj_att, w_out=w_out, norm_ffn_g=norm_ffn_g, w_router=w_router,
             w_exp_gate=w_exp_gate, w_exp_up=w_exp_up, w_exp_down=w_exp_down)
    tables = _rotary_lane_tables(seq)
    x2 = x.reshape(batch * seq, D_MODEL)
    for layer in range(depth):
        x2 = _mixer(x2, layer, p, tables, batch, seq)
        j = layer // 2
        if layer % 2 == 0:
            x2 = _ffn(x2, norm_ffn_g[layer][None], w_dense_gate[j].astype(BF16), w_dense_up[j].astype(BF16),
                      w_dense_down[j].astype(BF16))
        else:
            assert layer == depth - 1
            x2 = _moe_layer(x2, j, p, final_norm_g)
    return x2.reshape(batch, seq, D_MODEL)
```

```python
import functools

import jax
import jax.numpy as jnp
import numpy as np
from jax import lax
from jax.experimental import pallas as pl
from jax.experimental.pallas import tpu as pltpu

D_MODEL = 1024
D_RNN = D_MODEL
N_LRU_BLOCKS = 8
LRU_BLOCK = D_RNN // N_LRU_BLOCKS
CONV_WIDTH = 4
LRU_C = 8.0
HEAD_DIM = 64
HEADS_PER_GROUP = 4
WINDOWS = (128, 512, 2048)
DILATIONS = (1, 4, 16)
N_GROUPS = len(WINDOWS)
ATT_WIDTH = N_GROUPS * HEADS_PER_GROUP * HEAD_DIM
ATT_OUT = HEADS_PER_GROUP * HEAD_DIM
ROT_DIM = HEAD_DIM // 4
ROPE_THETA = 500000.0
HALF_KEYS = 64
IN_WIDTH = 2 * D_RNN + 3 * ATT_WIDTH + 2 * D_MODEL
QKV_WIDTH = 3 * ATT_WIDTH
YG_WIDTH = 3 * D_MODEL
N_EXPERTS = 8
TOP_K = 2
RMS_EPS = 1e-6

LANES = 128
SUBLANES = 8
VMEM_LIMIT = 52 * 1024 * 1024

TM = 512
LRU_TS = 256
LRU_CB = 256
ATT_CQ = 128
ATT_KW = ATT_CQ + 2 * HALF_KEYS
MOE_TM = 512
MOE_FC = 512
GATHER_TM = 256

BF16 = jnp.bfloat16
F32 = jnp.float32


def _cparams(sem):
    return pltpu.CompilerParams(dimension_semantics=sem, vmem_limit_bytes=VMEM_LIMIT)


def _resident(shape, index_map):
    return pl.BlockSpec(shape, index_map, pipeline_mode=pl.Buffered(1))


def _sigmoid(x):
    return 0.5 * jnp.tanh(0.5 * x) + 0.5


def _rms(x, g):
    ms = jnp.mean(x * x, axis=-1, keepdims=True)
    return x * lax.rsqrt(ms + RMS_EPS) * g


def _in_proj_kernel(x_ref, g_ref, w_ref, cos_ref, sa_ref, sb_ref, xr_ref, yg_ref, qkv_ref):
    hn = _rms(x_ref[...], g_ref[...]).astype(BF16)

    def proj(c0, width):
        return jnp.dot(hn, w_ref[:, c0:c0 + width], preferred_element_type=F32)

    for c in range(D_RNN // 512):
        xr_ref[:, c * 512:(c + 1) * 512] = proj(c * 512, 512)
    for c in range(YG_WIDTH // 512):
        yg_ref[:, c * 512:(c + 1) * 512] = proj(D_RNN + c * 512, 512).astype(BF16)

    cosf, sina, sinb = cos_ref[...], sa_ref[...], sb_ref[...]
    base = D_RNN + YG_WIDTH
    for c in range(QKV_WIDTH // 256):
        acc = proj(base + c * 256, 256)
        if c < 2 * ATT_WIDTH // 256:
            scale = HEAD_DIM ** -0.5 if c < ATT_WIDTH // 256 else 1.0
            for h in range(2):
                t = acc[:, h * LANES:(h + 1) * LANES]
                r = t * cosf + pltpu.roll(t, 8, 1) * sina + pltpu.roll(t, LANES - 8, 1) * sinb
                qkv_ref[2 * c + h] = r * scale
        else:
            for h in range(2):
                qkv_ref[2 * c + h] = acc[:, h * LANES:(h + 1) * LANES]


def _in_proj(x2, g, w_perm, cosf, sina, sinb, batch, seq):
    tokens = batch * seq
    ns = seq // TM
    return pl.pallas_call(
        _in_proj_kernel,
        grid=(tokens // TM,),
        in_specs=[
            pl.BlockSpec((TM, D_MODEL), lambda i: (i, 0)),
            _resident((1, D_MODEL), lambda i: (0, 0)),
            _resident((D_MODEL, IN_WIDTH), lambda i: (0, 0)),
            pl.BlockSpec((TM, LANES), lambda i: (i % ns, 0)),
            pl.BlockSpec((TM, LANES), lambda i: (i % ns, 0)),
            pl.BlockSpec((TM, LANES), lambda i: (i % ns, 0)),
        ],
        out_specs=[
            pl.BlockSpec((TM, D_RNN), lambda i: (i, 0)),
            pl.BlockSpec((TM, YG_WIDTH), lambda i: (i, 0)),
            pl.BlockSpec((None, QKV_WIDTH // LANES, TM, LANES), lambda i: (i // ns, 0, i % ns, 0)),
        ],
        out_shape=[
            jax.ShapeDtypeStruct((tokens, D_RNN), F32),
            jax.ShapeDtypeStruct((tokens, YG_WIDTH), BF16),
            jax.ShapeDtypeStruct((batch, QKV_WIDTH // LANES, seq, LANES), F32),
        ],
        compiler_params=_cparams(("parallel",)),
        name="in_proj",
    )(x2, g, w_perm, cosf, sina, sinb)


def _lru_kernel(xf_ref, xfp_ref, xfn_ref, xb_ref, xbp_ref, xbn_ref, cw_ref, cb_ref, wg_ref,
                br_ref, bi_ref, lam_ref, hf_ref, hb_ref, xp_scr, a_scr, u_scr, h_scr, carry_scr):
    c = pl.program_id(1)
    nc = pl.num_programs(1)
    nb = xf_ref.shape[0]
    ts = xf_ref.shape[1]

    @pl.when(c == 0)
    def _():
        carry_scr[...] = jnp.zeros_like(carry_scr)

    def time_major(v):
        return pltpu.einshape("bsc->sbc", v)

    def gates(direction, x_ref, prev_ref, next_ref, chunk):
        halo_p = time_major(prev_ref[...])
        halo_n = time_major(next_ref[...])
        xp_scr[0] = jnp.where(chunk > 0, halo_p[SUBLANES - 1], 0.0)
        xp_scr[1:ts + 1] = time_major(x_ref[...])
        xp_scr[ts + 1] = jnp.where(chunk < nc - 1, halo_n[0], 0.0)
        xp_scr[ts + 2] = jnp.where(chunk < nc - 1, halo_n[1], 0.0)
        cw = cw_ref[...]
        xc = cb_ref[...][None]
        for j in range(CONV_WIDTH):
            xc = xc + xp_scr[j:j + ts] * cw[j:j + 1][None]
        for j in range(LRU_CB // LRU_BLOCK):
            sl = slice(j * LRU_BLOCK, (j + 1) * LRU_BLOCK)
            xcb = xc[:, :, sl].reshape(ts * nb, LRU_BLOCK)
            rg = jnp.dot(xcb.astype(BF16), wg_ref[direction, j], preferred_element_type=F32)
            r = _sigmoid(rg[:, :LRU_BLOCK] + br_ref[direction:direction + 1, sl])
            ig = _sigmoid(rg[:, LRU_BLOCK:] + bi_ref[direction:direction + 1, sl])
            lam = lam_ref[direction:direction + 1, sl]
            kneg = -LRU_C * (jnp.maximum(-lam, 0.0) + jnp.log(1.0 + jnp.exp(-jnp.abs(lam))))
            a = jnp.exp(r * kneg)
            u = jnp.sqrt(1.0 - a * a) * (ig * xcb)
            a_scr[direction, :, :, sl] = a.reshape(ts, nb, LRU_BLOCK)
            u_scr[direction, :, :, sl] = u.reshape(ts, nb, LRU_BLOCK)

    gates(0, xf_ref, xfp_ref, xfn_ref, c)
    gates(1, xb_ref, xbp_ref, xbn_ref, nc - 1 - c)

    def step(t, carry):
        hf, hb = carry
        tb = ts - 1 - t
        hf = a_scr[0, t] * hf + u_scr[0, t]
        hb = a_scr[1, tb] * hb + u_scr[1, tb]
        h_scr[0, t] = hf
        h_scr[1, tb] = hb
        return hf, hb

    hf, hb = lax.fori_loop(0, ts, step, (carry_scr[0], carry_scr[1]), unroll=8)
    carry_scr[0] = hf
    carry_scr[1] = hb
    hf_ref[...] = pltpu.einshape("sbc->bsc", h_scr[0]).astype(hf_ref.dtype)
    hb_ref[...] = pltpu.einshape("sbc->bsc", h_scr[1]).astype(hb_ref.dtype)


def _lru(xr3, conv_w, conv_b, wgate, b_r, b_i, lam):
    batch, seq, _ = xr3.shape
    nc = seq // LRU_TS
    ncb = D_RNN // LRU_CB
    hblk = LRU_TS // SUBLANES
    nh = seq // SUBLANES

    def cur(rev):
        return lambda n, c: (0, (nc - 1 - c) if rev else c, n)

    def prev(rev):
        return lambda n, c: (0, jnp.maximum(((nc - 1 - c) if rev else c) * hblk - 1, 0), n)

    def nxt(rev):
        return lambda n, c: (0, jnp.minimum((((nc - 1 - c) if rev else c) + 1) * hblk, nh - 1), n)

    main = lambda rev: pl.BlockSpec((batch, LRU_TS, LRU_CB), cur(rev))
    halo_p = lambda rev: pl.BlockSpec((batch, SUBLANES, LRU_CB), prev(rev))
    halo_n = lambda rev: pl.BlockSpec((batch, SUBLANES, LRU_CB), nxt(rev))
    vec2 = pl.BlockSpec((2, LRU_CB), lambda n, c: (0, n))
    out_sds = jax.ShapeDtypeStruct((batch, seq, D_RNN), BF16)
    return pl.pallas_call(
        _lru_kernel,
        grid=(ncb, nc),
        in_specs=[
            main(False), halo_p(False), halo_n(False),
            main(True), halo_p(True), halo_n(True),
            pl.BlockSpec((CONV_WIDTH, LRU_CB), lambda n, c: (0, n)),
            pl.BlockSpec((1, LRU_CB), lambda n, c: (0, n)),
            pl.BlockSpec((2, LRU_CB // LRU_BLOCK, LRU_BLOCK, 2 * LRU_BLOCK), lambda n, c: (0, n, 0, 0)),
            vec2, vec2, vec2,
        ],
        out_specs=[main(False), main(True)],
        out_shape=[out_sds, out_sds],
        scratch_shapes=[
            pltpu.VMEM((LRU_TS + CONV_WIDTH - 1, batch, LRU_CB), F32),
            pltpu.VMEM((2, LRU_TS, batch, LRU_CB), F32),
            pltpu.VMEM((2, LRU_TS, batch, LRU_CB), F32),
            pltpu.VMEM((2, LRU_TS, batch, LRU_CB), F32),
            pltpu.VMEM((2, batch, LRU_CB), F32),
        ],
        compiler_params=_cparams(("parallel", "arbitrary")),
        name="lru",
    )(xr3, xr3, xr3, xr3, xr3, xr3, conv_w, conv_b, wgate, b_r, b_i, lam)


def _attn_kernel(q_ref, k_ref, v_ref, out_ref, acc_o, acc_l):
    g = pl.program_id(1)
    seq = q_ref.shape[1]
    nblk = ATT_OUT // LANES
    rows4 = HEADS_PER_GROUP * ATT_CQ
    lane_head = lax.broadcasted_iota(jnp.int32, (1, ATT_OUT), 1) // HEAD_DIM
    rowq = lax.broadcasted_iota(jnp.int32, (rows4, ATT_KW), 0) & (ATT_CQ - 1)
    col = lax.broadcasted_iota(jnp.int32, (rows4, ATT_KW), 1)

    def rows(start, n, d):
        return pl.ds(start, n) if d == 1 else pl.ds(start, n, stride=d)

    def load(ref, start, n, d):
        return jnp.concatenate([ref[j, rows(start, n, d), :] for j in range(nblk)], axis=1)

    def store(ref, start, n, d, val):
        for j in range(nblk):
            ref[j, rows(start, n, d), :] = val[:, j * LANES:(j + 1) * LANES]

    def run_group(gi):
        d = DILATIONS[gi]
        length = seq // d
        chunks_per_residue = length // ATT_CQ

        def chunk(n, _):
            r = n // chunks_per_residue
            qs = (n - r * chunks_per_residue) * ATT_CQ
            ks = jnp.clip(qs - HALF_KEYS, 0, length - ATT_KW)
            q0, k0 = r + d * qs, r + d * ks
            if d == 1:
                q0, k0 = pl.multiple_of(q0, ATT_CQ), pl.multiple_of(k0, HALF_KEYS)
            q = load(q_ref, q0, ATT_CQ, d).astype(BF16)
            kk = load(k_ref, k0, ATT_KW, d).astype(BF16)
            vv = load(v_ref, k0, ATT_KW, d).astype(BF16)
            q4 = jnp.concatenate([jnp.where(lane_head == h, q, jnp.zeros_like(q))
                                  for h in range(HEADS_PER_GROUP)], axis=0)
            s = lax.dot_general(q4, kk, (((1,), (1,)), ((), ())), preferred_element_type=F32)
            band = jnp.abs((col + ks) - (rowq + qs)) <= HALF_KEYS
            s = jnp.where(band, s, -jnp.inf)
            m = jnp.max(s, axis=1, keepdims=True)
            p = jnp.exp(s - m)
            l = jnp.sum(p, axis=1, keepdims=True)
            pv = jnp.dot(p.astype(BF16), vv, preferred_element_type=F32)
            inv = 1.0 / l
            lse = m + jnp.log(l)
            o = jnp.zeros((ATT_CQ, ATT_OUT), F32)
            lf = jnp.zeros((ATT_CQ, ATT_OUT), F32)
            for h in range(HEADS_PER_GROUP):
                sl = slice(h * ATT_CQ, (h + 1) * ATT_CQ)
                sel = lane_head == h
                o = jnp.where(sel, pv[sl] * inv[sl], o)
                lf = jnp.where(sel, lse[sl], lf)
            if gi == 0:
                store(acc_o, q0, ATT_CQ, d, o)
                store(acc_l, q0, ATT_CQ, d, lf)
            else:
                ao = load(acc_o, q0, ATT_CQ, d)
                al = load(acc_l, q0, ATT_CQ, d)
                top = jnp.maximum(al, lf)
                wa, wg = jnp.exp(al - top), jnp.exp(lf - top)
                den = wa + wg
                store(acc_o, q0, ATT_CQ, d, (wa * ao + wg * o) * (1.0 / den))
                if gi < N_GROUPS - 1:
                    store(acc_l, q0, ATT_CQ, d, top + jnp.log(den))
            return 0

        lax.fori_loop(0, seq // ATT_CQ, chunk, 0)

    for gi in range(N_GROUPS):
        pl.when(g == gi)(functools.partial(run_group, gi))

    @pl.when(g == N_GROUPS - 1)
    def _():
        out_ref[...] = jnp.concatenate([acc_o[j] for j in range(nblk)], axis=1).astype(out_ref.dtype)


def _attention(qkv4, batch, seq):
    per_kind = ATT_WIDTH // ATT_OUT
    nblk = ATT_OUT // LANES

    def spec(kind):
        return pl.BlockSpec((None, nblk, seq, LANES), lambda b, g: (b, kind * per_kind + g, 0, 0))

    return pl.pallas_call(
        _attn_kernel,
        grid=(batch, N_GROUPS),
        in_specs=[spec(0), spec(1), spec(2)],
        out_specs=pl.BlockSpec((seq, ATT_OUT), lambda b, g: (b, 0)),
        out_shape=jax.ShapeDtypeStruct((batch * seq, ATT_OUT), BF16),
        scratch_shapes=[pltpu.VMEM((nblk, seq, LANES), F32), pltpu.VMEM((nblk, seq, LANES), F32)],
        compiler_params=_cparams(("parallel", "arbitrary")),
        name="attention",
    )(qkv4, qkv4, qkv4)


def _merge_kernel(x_ref, hf_ref, hb_ref, yg_ref, att_ref, wpl_ref, wpa_ref, wo_ref, out_ref):
    y = yg_ref[:, :D_MODEL].astype(F32)
    gelu = 0.5 * y * (1.0 + jnp.tanh(np.sqrt(2.0 / np.pi) * (y + 0.044715 * (y * y * y))))
    lru = ((hf_ref[...].astype(F32) + hb_ref[...].astype(F32)) * gelu).astype(BF16)
    branch_lru = jnp.dot(lru, wpl_ref[...], preferred_element_type=F32)
    branch_att = jnp.dot(att_ref[...], wpa_ref[...], preferred_element_type=F32)

    g_lru = _sigmoid(yg_ref[:, D_MODEL:2 * D_MODEL].astype(F32))
    g_att = _sigmoid(yg_ref[:, 2 * D_MODEL:].astype(F32))
    merged = (g_lru * branch_lru + g_att * branch_att).astype(BF16)
    out_ref[...] = x_ref[...] + jnp.dot(merged, wo_ref[...], preferred_element_type=F32)


def _merge(x2, hf2, hb2, yg, att, wpl, wpa, wo):
    tokens = x2.shape[0]
    tok = lambda w: pl.BlockSpec((TM, w), lambda i: (i, 0))
    return pl.pallas_call(
        _merge_kernel,
        grid=(tokens // TM,),
        in_specs=[tok(D_MODEL), tok(D_RNN), tok(D_RNN), tok(YG_WIDTH), tok(ATT_OUT)] + [
            _resident((D_RNN, D_MODEL), lambda i: (0, 0)),
            _resident((ATT_OUT, D_MODEL), lambda i: (0, 0)),
            _resident((D_MODEL, D_MODEL), lambda i: (0, 0)),
        ],
        out_specs=tok(D_MODEL),
        out_shape=jax.ShapeDtypeStruct((tokens, D_MODEL), F32),
        compiler_params=_cparams(("parallel",)),
        name="merge",
    )(x2, hf2, hb2, yg, att, wpl, wpa, wo)


def _ffn_chunks(d_ff):
    tiles = d_ff // 256
    sizes = [tiles // 3 + (1 if k < tiles % 3 else 0) for k in range(3)]
    out, start = [], 0
    for s in sizes:
        if s:
            out.append((start * 256, s * 256))
            start += s
    return out


def _ffn_kernel(x_ref, g_ref, wg_ref, wu_ref, wd_ref, out_ref):
    x = x_ref[...]
    hn = _rms(x, g_ref[...]).astype(BF16)
    acc = x
    for c0, width in _ffn_chunks(wg_ref.shape[1]):
        gate = jnp.dot(hn, wg_ref[:, c0:c0 + width], preferred_element_type=F32)
        up = jnp.dot(hn, wu_ref[:, c0:c0 + width], preferred_element_type=F32)
        act = (gate * _sigmoid(gate) * up).astype(BF16)
        acc = acc + jnp.dot(act, wd_ref[c0:c0 + width, :], preferred_element_type=F32)
    out_ref[...] = acc


def _ffn(x2, g, wg, wu, wd):
    tokens = x2.shape[0]
    d_ff = wg.shape[1]
    return pl.pallas_call(
        _ffn_kernel,
        grid=(tokens // TM,),
        in_specs=[
            pl.BlockSpec((TM, D_MODEL), lambda i: (i, 0)),
            _resident((1, D_MODEL), lambda i: (0, 0)),
            _resident((D_MODEL, d_ff), lambda i: (0, 0)),
            _resident((D_MODEL, d_ff), lambda i: (0, 0)),
            _resident((d_ff, D_MODEL), lambda i: (0, 0)),
        ],
        out_specs=pl.BlockSpec((TM, D_MODEL), lambda i: (i, 0)),
        out_shape=jax.ShapeDtypeStruct((tokens, D_MODEL), F32),
        compiler_params=_cparams(("parallel",)),
        name="ffn_dense",
    )(x2, g, wg, wu, wd)


def _router_kernel(x_ref, g_ref, whi_ref, wlo_ref, tri_ref, hp_ref, wc_ref, meta_ref, cnt_ref, carry_scr):
    i = pl.program_id(0)

    @pl.when(i == 0)
    def _():
        carry_scr[...] = jnp.zeros_like(carry_scr)

    hn = _rms(x_ref[...], g_ref[...])
    hi = hn.astype(BF16)
    lo = (hn - hi.astype(F32)).astype(BF16)
    hp_ref[...] = hn

    whi = whi_ref[...]
    logits = (jnp.dot(hi, whi, preferred_element_type=F32) + jnp.dot(lo, whi, preferred_element_type=F32)
              + jnp.dot(hi, wlo_ref[...], preferred_element_type=F32))
    tm = logits.shape[0]
    lane = lax.broadcasted_iota(jnp.int32, (tm, LANES), 1).astype(F32)
    lg = jnp.where(lane < N_EXPERTS, logits, -jnp.inf)
    m1 = jnp.max(lg, axis=1, keepdims=True)
    i1 = jnp.min(jnp.where(lg == m1, lane, float(LANES)), axis=1, keepdims=True)
    lg2 = jnp.where(lane == i1, -jnp.inf, lg)
    m2 = jnp.max(lg2, axis=1, keepdims=True)
    i2 = jnp.min(jnp.where(lg2 == m2, lane, float(LANES)), axis=1, keepdims=True)
    e = jnp.exp(m2 - m1)
    w1 = 1.0 / (1.0 + e)
    w2 = e * w1
    wc_ref[...] = jnp.where(lane == 0.0, w1, jnp.where(lane == 1.0, w2, 0.0))

    hot1 = lane == i1
    hot2 = lane == i2
    onehot = jnp.where(hot1 | hot2, 1.0, 0.0).astype(BF16)
    cum = jnp.dot(tri_ref[...], onehot, preferred_element_type=F32)
    tot = carry_scr[...] + cum
    r1 = jnp.sum(jnp.where(hot1, tot - 1.0, 0.0), axis=1, keepdims=True)
    r2 = jnp.sum(jnp.where(hot2, tot - 1.0, 0.0), axis=1, keepdims=True)
    carry_scr[...] = tot[tm - 1:tm, :]
    cnt_ref[...] = tot[tm - 1:tm, :].astype(jnp.int32)

    cols = jnp.where(lane == 0.0, i1, jnp.where(lane == 1.0, i2, jnp.where(lane == 2.0, r1,
                     jnp.where(lane == 3.0, r2, 0.0))))
    meta_ref[...] = cols.T[:SUBLANES, :].astype(jnp.int32)


def _router(x2, g, whi, wlo):
    tokens = x2.shape[0]
    tri = jnp.tril(jnp.ones((TM, TM), F32)).astype(BF16)
    return pl.pallas_call(
        _router_kernel,
        grid=(tokens // TM,),
        in_specs=[
            pl.BlockSpec((TM, D_MODEL), lambda i: (i, 0)),
            _resident((1, D_MODEL), lambda i: (0, 0)),
            _resident((D_MODEL, LANES), lambda i: (0, 0)),
            _resident((D_MODEL, LANES), lambda i: (0, 0)),
            _resident((TM, TM), lambda i: (0, 0)),
        ],
        out_specs=[
            pl.BlockSpec((TM, D_MODEL), lambda i: (i, 0)),
            pl.BlockSpec((TM, LANES), lambda i: (i, 0)),
            pl.BlockSpec((SUBLANES, TM), lambda i: (0, i)),
            pl.BlockSpec((1, LANES), lambda i: (0, 0)),
        ],
        out_shape=[
            jax.ShapeDtypeStruct((tokens, D_MODEL), F32),
            jax.ShapeDtypeStruct((tokens, LANES), F32),
            jax.ShapeDtypeStruct((SUBLANES, tokens), jnp.int32),
            jax.ShapeDtypeStruct((1, LANES), jnp.int32),
        ],
        scratch_shapes=[pltpu.VMEM((1, LANES), F32)],
        compiler_params=_cparams(("arbitrary",)),
        name="router",
    )(x2, g, whi, wlo, tri)


def _dispatch_kernel(pos_ref, hp_ref, xs_in_ref, xs_ref, sem):
    del xs_in_ref
    tm = hp_ref.shape[0]

    def copy(r, k):
        return pltpu.make_async_copy(hp_ref.at[pl.ds(r, 1)], xs_ref.at[pl.ds(pos_ref[k, r], 1)], sem)

    def start(r, _):
        copy(r, 0).start()
        copy(r, 1).start()
        return 0

    def wait(r, _):
        copy(r, 0).wait()
        copy(r, 1).wait()
        return 0

    lax.fori_loop(0, tm, start, 0, unroll=8)
    lax.fori_loop(0, tm, wait, 0, unroll=8)


def _dispatch(pos3, hp, rows):
    tokens = hp.shape[0]
    xs0 = jnp.zeros((rows, D_MODEL), F32)
    return pl.pallas_call(
        _dispatch_kernel,
        grid=(tokens // GATHER_TM,),
        in_specs=[
            pl.BlockSpec((None, TOP_K, GATHER_TM), lambda i: (i, 0, 0), memory_space=pltpu.SMEM),
            pl.BlockSpec((GATHER_TM, D_MODEL), lambda i: (i, 0)),
            pl.BlockSpec(memory_space=pl.ANY),
        ],
        out_specs=pl.BlockSpec(memory_space=pl.ANY),
        out_shape=jax.ShapeDtypeStruct((rows, D_MODEL), F32),
        scratch_shapes=[pltpu.SemaphoreType.DMA],
        input_output_aliases={2: 0},
        compiler_params=_cparams(("arbitrary",)),
        name="dispatch",
    )(pos3, hp, xs0)


def _moe_kernel(te_ref, tv_ref, xs_ref, wg_ref, wu_ref, wd_ref, ys_ref):
    i = pl.program_id(0)

    @pl.when(tv_ref[i] == 0)
    def _():
        ys_ref[...] = jnp.zeros_like(ys_ref)

    @pl.when(tv_ref[i] != 0)
    def _():
        x = xs_ref[...].astype(BF16)
        acc = jnp.zeros(ys_ref.shape, F32)
        for c in range(wg_ref.shape[1] // MOE_FC):
            cs = slice(c * MOE_FC, (c + 1) * MOE_FC)
            gate = jnp.dot(x, wg_ref[:, cs], preferred_element_type=F32)
            up = jnp.dot(x, wu_ref[:, cs], preferred_element_type=F32)
            act = (gate * _sigmoid(gate) * up).astype(BF16)
            acc = acc + jnp.dot(act, wd_ref[cs, :], preferred_element_type=F32)
        ys_ref[...] = acc


def _moe(tile_expert, tile_valid, xs, wg, wu, wd):
    rows = xs.shape[0]
    d_exp = wg.shape[2]
    grid_spec = pltpu.PrefetchScalarGridSpec(
        num_scalar_prefetch=2,
        grid=(rows // MOE_TM,),
        in_specs=[
            pl.BlockSpec((MOE_TM, D_MODEL), lambda i, te, tv: (i, 0)),
            pl.BlockSpec((None, D_MODEL, d_exp), lambda i, te, tv: (te[i], 0, 0), pipeline_mode=pl.Buffered(1)),
            pl.BlockSpec((None, D_MODEL, d_exp), lambda i, te, tv: (te[i], 0, 0), pipeline_mode=pl.Buffered(1)),
            pl.BlockSpec((None, d_exp, D_MODEL), lambda i, te, tv: (te[i], 0, 0), pipeline_mode=pl.Buffered(1)),
        ],
        out_specs=pl.BlockSpec((MOE_TM, D_MODEL), lambda i, te, tv: (i, 0)),
    )
    return pl.pallas_call(
        _moe_kernel,
        grid_spec=grid_spec,
        out_shape=jax.ShapeDtypeStruct((rows, D_MODEL), F32),
        compiler_params=_cparams(("arbitrary",)),
        name="experts",
    )(tile_expert, tile_valid, xs, wg, wu, wd)


def _combine_kernel(pos_ref, x_ref, wc_ref, g_ref, ys_ref, out_ref, buf, sem):
    tm = x_ref.shape[0]

    def copy(r, k):
        return pltpu.make_async_copy(ys_ref.at[pl.ds(pos_ref[k, r], 1)], buf.at[k, pl.ds(r, 1)], sem)

    def start(r, _):
        copy(r, 0).start()
        copy(r, 1).start()
        return 0

    def wait(r, _):
        copy(r, 0).wait()
        copy(r, 1).wait()
        return 0

    lax.fori_loop(0, tm, start, 0, unroll=8)
    lax.fori_loop(0, tm, wait, 0, unroll=8)
    wc = wc_ref[...]
    y = x_ref[...] + (wc[:, 0:1] * buf[0] + wc[:, 1:2] * buf[1])
    out_ref[...] = _rms(y, g_ref[...])


def _combine(pos3, x2, wcols, g, ys):
    tokens = x2.shape[0]
    return pl.pallas_call(
        _combine_kernel,
        grid=(tokens // GATHER_TM,),
        in_specs=[
            pl.BlockSpec((None, TOP_K, GATHER_TM), lambda i: (i, 0, 0), memory_space=pltpu.SMEM),
            pl.BlockSpec((GATHER_TM, D_MODEL), lambda i: (i, 0)),
            pl.BlockSpec((GATHER_TM, LANES), lambda i: (i, 0)),
            _resident((1, D_MODEL), lambda i: (0, 0)),
            pl.BlockSpec(memory_space=pl.ANY),
        ],
        out_specs=pl.BlockSpec((GATHER_TM, D_MODEL), lambda i: (i, 0)),
        out_shape=jax.ShapeDtypeStruct((tokens, D_MODEL), F32),
        scratch_shapes=[pltpu.VMEM((TOP_K, GATHER_TM, D_MODEL), F32), pltpu.SemaphoreType.DMA],
        compiler_params=_cparams(("arbitrary",)),
        name="combine",
    )(pos3, x2, wcols, g, ys)


def _rotary_lane_tables(seq):
    half = ROT_DIM // 2
    pos = jnp.arange(seq, dtype=F32)
    inv_freq = ROPE_THETA ** (-jnp.arange(0, ROT_DIM, 2, dtype=F32) / ROT_DIM)
    ang = pos[:, None] * inv_freq[None, :]
    cos, sin = jnp.cos(ang), jnp.sin(ang)
    j = np.arange(LANES) % HEAD_DIM
    idx = j % half
    cosf = jnp.where((j < ROT_DIM)[None, :], cos[:, idx], 1.0)
    sina = jnp.where(((j >= half) & (j < ROT_DIM))[None, :], sin[:, idx], 0.0)
    sinb = jnp.where((j < half)[None, :], -sin[:, idx], 0.0)
    return cosf, sina, sinb


def _mixer(x2, layer, p, tables, batch, seq):
    cuts = np.cumsum([D_RNN, D_RNN, ATT_WIDTH, ATT_WIDTH, ATT_WIDTH])
    w = p["w_in"][layer]
    w_perm = jnp.concatenate([w[:, :cuts[1]], w[:, cuts[4]:], w[:, cuts[1]:cuts[4]]], axis=1).astype(BF16)
    xr, yg, qkv4 = _in_proj(x2, p["norm_mix_g"][layer][None], w_perm, *tables, batch, seq)

    wgate = jnp.concatenate([p["w_rgate"][layer], p["w_igate"][layer]], axis=-1).astype(BF16)
    hf, hb = _lru(xr.reshape(batch, seq, D_RNN), p["conv_w"][layer], p["conv_b"][layer][None], wgate,
                  p["b_rgate"][layer], p["b_igate"][layer], p["lru_lambda"][layer])
    att = _attention(qkv4, batch, seq)

    tokens = batch * seq
    return _merge(x2, hf.reshape(tokens, D_RNN), hb.reshape(tokens, D_RNN), yg, att,
                  p["w_proj_lru"][layer].astype(BF16), p["w_proj_att"][layer].astype(BF16),
                  p["w_out"][layer].astype(BF16))


def _moe_layer(x2, j, p, final_g):
    tokens = x2.shape[0]
    wr = jnp.pad(p["w_router"][j], ((0, 0), (0, LANES - N_EXPERTS)))
    whi = wr.astype(BF16)
    wlo = (wr - whi.astype(F32)).astype(BF16)
    hp, wcols, meta, counts = _router(x2, p["norm_ffn_g"][j * 2 + 1][None], whi, wlo)

    counts = counts[0, :N_EXPERTS]
    padded = ((counts + MOE_TM - 1) // MOE_TM) * MOE_TM
    ends = jnp.cumsum(padded)
    starts = ends - padded
    experts_of = meta[0:TOP_K]
    base = jnp.zeros_like(experts_of)
    for e in range(N_EXPERTS):
        base = base + jnp.where(experts_of == e, starts[e], 0)
    pos = base + meta[TOP_K:2 * TOP_K]
    pos3 = pos.reshape(TOP_K, tokens // GATHER_TM, GATHER_TM).transpose(1, 0, 2)

    rows = TOP_K * tokens + N_EXPERTS * MOE_TM
    tile_start = jnp.arange(rows // MOE_TM, dtype=jnp.int32) * MOE_TM
    tile_expert = jnp.minimum(jnp.sum(tile_start[:, None] >= ends[None, :], axis=1), N_EXPERTS - 1).astype(jnp.int32)
    tile_valid = (tile_start < ends[-1]).astype(jnp.int32)

    xs = _dispatch(pos3, hp, rows)
    ys = _moe(tile_expert, tile_valid, xs, p["w_exp_gate"][j].astype(BF16), p["w_exp_up"][j].astype(BF16),
              p["w_exp_down"][j].astype(BF16))
    return _combine(pos3, x2, wcols, final_g[None], ys)


def kernel(x, norm_mix_g, w_in, conv_w, conv_b, w_rgate, b_rgate, w_igate, b_igate, lru_lambda, w_proj_lru,
           w_proj_att, w_out, norm_ffn_g, w_dense_gate, w_dense_up, w_dense_down, w_router, w_exp_gate, w_exp_up,
           w_exp_down, final_norm_g):
    batch, seq, _ = x.shape
    depth = w_in.shape[0]
    assert depth % 2 == 0, "the final RMSNorm is fused into the last (routed) layer"
    p = dict(norm_mix_g=norm_mix_g, w_in=w_in, conv_w=conv_w, conv_b=conv_b, w_rgate=w_rgate, b_rgate=b_rgate,
             w_igate=w_igate, b_igate=b_igate, lru_lambda=lru_lambda, w_proj_lru=w_proj_lru,
             w_proj_att=w_proj_att, w_out=w_out, norm_ffn_g=norm_ffn_g, w_router=w_router,
             w_exp_gate=w_exp_gate, w_exp_up=w_exp_up, w_exp_down=w_exp_down)
    tables = _rotary_lane_tables(seq)
    x2 = x.reshape(batch * seq, D_MODEL)
    for layer in range(depth):
        x2 = _mixer(x2, layer, p, tables, batch, seq)
        j = layer // 2
        if layer % 2 == 0:
            x2 = _ffn(x2, norm_ffn_g[layer][None], w_dense_gate[j].astype(BF16), w_dense_up[j].astype(BF16),
                      w_dense_down[j].astype(BF16))
        else:
            assert layer == depth - 1
            x2 = _moe_layer(x2, j, p, final_norm_g)
    return x2.reshape(batch, seq, D_MODEL)
```

```python
import functools

import jax
import jax.numpy as jnp
import numpy as np
from jax import lax
from jax.experimental import pallas as pl
from jax.experimental.pallas import tpu as pltpu

D_MODEL = 1024
D_RNN = D_MODEL
N_LRU_BLOCKS = 8
LRU_BLOCK = D_RNN // N_LRU_BLOCKS
CONV_WIDTH = 4
LRU_C = 8.0
HEAD_DIM = 64
HEADS_PER_GROUP = 4
WINDOWS = (128, 512, 2048)
DILATIONS = (1, 4, 16)
N_GROUPS = len(WINDOWS)
ATT_WIDTH = N_GROUPS * HEADS_PER_GROUP * HEAD_DIM
ATT_OUT = HEADS_PER_GROUP * HEAD_DIM
ROT_DIM = HEAD_DIM // 4
ROPE_THETA = 500000.0
HALF_KEYS = 64
IN_WIDTH = 2 * D_RNN + 3 * ATT_WIDTH + 2 * D_MODEL
QKV_WIDTH = 3 * ATT_WIDTH
YG_WIDTH = 3 * D_MODEL
N_EXPERTS = 8
TOP_K = 2
RMS_EPS = 1e-6

LANES = 128
SUBLANES = 8
VMEM_LIMIT = 52 * 1024 * 1024

TM = 512
LRU_TS = 256
LRU_CB = 256
ATT_CQ = 128
ATT_KW = ATT_CQ + 2 * HALF_KEYS
MOE_TM = 512
MOE_FC = 512
GATHER_TM = 256

BF16 = jnp.bfloat16
F32 = jnp.float32
LOG2E = float(np.log2(np.e))
LN2 = float(np.log(2.0))
Q_SCALE = HEAD_DIM ** -0.5 * LOG2E
SQRT_FLOOR = 1e-30


def _cparams(sem):
    return pltpu.CompilerParams(dimension_semantics=sem, vmem_limit_bytes=VMEM_LIMIT)


def _resident(shape, index_map):
    return pl.BlockSpec(shape, index_map, pipeline_mode=pl.Buffered(1))


def _sigmoid(x):
    return 0.5 * jnp.tanh(0.5 * x) + 0.5


def _rms(x, g):
    ms = jnp.mean(x * x, axis=-1, keepdims=True)
    return x * lax.rsqrt(ms + RMS_EPS) * g


def _in_proj_kernel(x_ref, g_ref, w_ref, cos_ref, sa_ref, sb_ref, xr_ref, yg_ref, qkv_ref):
    hn = _rms(x_ref[...], g_ref[...]).astype(BF16)

    def proj(c0, width):
        return jnp.dot(hn, w_ref[:, c0:c0 + width], preferred_element_type=F32)

    for c in range(D_RNN // 512):
        xr_ref[:, c * 512:(c + 1) * 512] = proj(c * 512, 512)
    for c in range(YG_WIDTH // 512):
        yg_ref[:, c * 512:(c + 1) * 512] = proj(D_RNN + c * 512, 512).astype(BF16)

    cosf, sina, sinb = cos_ref[...], sa_ref[...], sb_ref[...]
    base = D_RNN + YG_WIDTH
    for c in range(QKV_WIDTH // 256):
        acc = proj(base + c * 256, 256)
        if c < 2 * ATT_WIDTH // 256:
            for h in range(2):
                t = acc[:, h * LANES:(h + 1) * LANES]
                r = t * cosf + pltpu.roll(t, 8, 1) * sina + pltpu.roll(t, LANES - 8, 1) * sinb
                qkv_ref[2 * c + h] = r * Q_SCALE if c < ATT_WIDTH // 256 else r
        else:
            for h in range(2):
                qkv_ref[2 * c + h] = acc[:, h * LANES:(h + 1) * LANES]


def _in_proj(x2, g, w_perm, cosf, sina, sinb, batch, seq):
    tokens = batch * seq
    ns = seq // TM
    return pl.pallas_call(
        _in_proj_kernel,
        grid=(tokens // TM,),
        in_specs=[
            pl.BlockSpec((TM, D_MODEL), lambda i: (i, 0)),
            _resident((1, D_MODEL), lambda i: (0, 0)),
            _resident((D_MODEL, IN_WIDTH), lambda i: (0, 0)),
            pl.BlockSpec((TM, LANES), lambda i: (i % ns, 0)),
            pl.BlockSpec((TM, LANES), lambda i: (i % ns, 0)),
            pl.BlockSpec((TM, LANES), lambda i: (i % ns, 0)),
        ],
        out_specs=[
            pl.BlockSpec((TM, D_RNN), lambda i: (i, 0)),
            pl.BlockSpec((TM, YG_WIDTH), lambda i: (i, 0)),
            pl.BlockSpec((None, QKV_WIDTH // LANES, TM, LANES), lambda i: (i // ns, 0, i % ns, 0)),
        ],
        out_shape=[
            jax.ShapeDtypeStruct((tokens, D_RNN), F32),
            jax.ShapeDtypeStruct((tokens, YG_WIDTH), BF16),
            jax.ShapeDtypeStruct((batch, QKV_WIDTH // LANES, seq, LANES), F32),
        ],
        compiler_params=_cparams(("parallel",)),
        name="in_proj",
    )(x2, g, w_perm, cosf, sina, sinb)


def _lru_kernel(xf_ref, xfp_ref, xfn_ref, xb_ref, xbp_ref, xbn_ref, cw_ref, cb_ref, wg_ref,
                br_ref, bi_ref, lam_ref, hf_ref, hb_ref, xp_scr, a_scr, u_scr, h_scr, carry_scr):
    c = pl.program_id(1)
    nc = pl.num_programs(1)
    nb = xf_ref.shape[0]
    ts = xf_ref.shape[1]

    @pl.when(c == 0)
    def _():
        carry_scr[...] = jnp.zeros_like(carry_scr)

    def time_major(v):
        return jnp.swapaxes(v, 0, 1)

    def gates(direction, x_ref, prev_ref, next_ref, chunk):
        halo_p = time_major(prev_ref[...])
        halo_n = time_major(next_ref[...])
        xp_scr[0] = jnp.where(chunk > 0, halo_p[SUBLANES - 1], 0.0)
        xp_scr[1:ts + 1] = time_major(x_ref[...])
        xp_scr[ts + 1] = jnp.where(chunk < nc - 1, halo_n[0], 0.0)
        xp_scr[ts + 2] = jnp.where(chunk < nc - 1, halo_n[1], 0.0)
        cw = cw_ref[...]
        xc = cb_ref[...][None]
        for j in range(CONV_WIDTH):
            xc = xc + xp_scr[j:j + ts] * cw[j:j + 1][None]
        for j in range(LRU_CB // LRU_BLOCK):
            sl = slice(j * LRU_BLOCK, (j + 1) * LRU_BLOCK)
            xcb = xc[:, :, sl].reshape(ts * nb, LRU_BLOCK)
            rg = jnp.dot(xcb.astype(BF16), wg_ref[direction, j], preferred_element_type=F32)
            tr = jnp.tanh(rg[:, :LRU_BLOCK] + br_ref[direction:direction + 1, sl])
            ti = jnp.tanh(rg[:, LRU_BLOCK:] + bi_ref[direction:direction + 1, sl])
            lam = lam_ref[direction:direction + 1, sl]
            c2 = (-0.5 * LRU_C * LOG2E) * (jnp.maximum(-lam, 0.0) + jnp.log(1.0 + jnp.exp(-jnp.abs(lam))))
            a = jnp.exp2(tr * c2 + c2)
            xh = 0.5 * xcb
            gated = ti * xh + xh
            om = 1.0 - a * a
            u = (om * lax.rsqrt(jnp.maximum(om, SQRT_FLOOR))) * gated
            a_scr[direction, :, :, sl] = a.reshape(ts, nb, LRU_BLOCK)
            u_scr[direction, :, :, sl] = u.reshape(ts, nb, LRU_BLOCK)

    gates(0, xf_ref, xfp_ref, xfn_ref, c)
    gates(1, xb_ref, xbp_ref, xbn_ref, nc - 1 - c)

    def step(t, carry):
        hf, hb = carry
        tb = ts - 1 - t
        hf = a_scr[0, t] * hf + u_scr[0, t]
        hb = a_scr[1, tb] * hb + u_scr[1, tb]
        h_scr[0, t] = hf
        h_scr[1, tb] = hb
        return hf, hb

    hf, hb = lax.fori_loop(0, ts, step, (carry_scr[0], carry_scr[1]), unroll=8)
    carry_scr[0] = hf
    carry_scr[1] = hb
    hf_ref[...] = jnp.swapaxes(h_scr[0], 0, 1).astype(hf_ref.dtype)
    hb_ref[...] = jnp.swapaxes(h_scr[1], 0, 1).astype(hb_ref.dtype)


def _lru(xr3, conv_w, conv_b, wgate, b_r, b_i, lam):
    batch, seq, _ = xr3.shape
    nc = seq // LRU_TS
    ncb = D_RNN // LRU_CB
    hblk = LRU_TS // SUBLANES
    nh = seq // SUBLANES

    def cur(rev):
        return lambda n, c: (0, (nc - 1 - c) if rev else c, n)

    def prev(rev):
        return lambda n, c: (0, jnp.maximum(((nc - 1 - c) if rev else c) * hblk - 1, 0), n)

    def nxt(rev):
        return lambda n, c: (0, jnp.minimum((((nc - 1 - c) if rev else c) + 1) * hblk, nh - 1), n)

    main = lambda rev: pl.BlockSpec((batch, LRU_TS, LRU_CB), cur(rev))
    halo_p = lambda rev: pl.BlockSpec((batch, SUBLANES, LRU_CB), prev(rev))
    halo_n = lambda rev: pl.BlockSpec((batch, SUBLANES, LRU_CB), nxt(rev))
    vec2 = pl.BlockSpec((2, LRU_CB), lambda n, c: (0, n))
    out_sds = jax.ShapeDtypeStruct((batch, seq, D_RNN), BF16)
    return pl.pallas_call(
        _lru_kernel,
        grid=(ncb, nc),
        in_specs=[
            main(False), halo_p(False), halo_n(False),
            main(True), halo_p(True), halo_n(True),
            pl.BlockSpec((CONV_WIDTH, LRU_CB), lambda n, c: (0, n)),
            pl.BlockSpec((1, LRU_CB), lambda n, c: (0, n)),
            pl.BlockSpec((2, LRU_CB // LRU_BLOCK, LRU_BLOCK, 2 * LRU_BLOCK), lambda n, c: (0, n, 0, 0)),
            vec2, vec2, vec2,
        ],
        out_specs=[main(False), main(True)],
        out_shape=[out_sds, out_sds],
        scratch_shapes=[
            pltpu.VMEM((LRU_TS + CONV_WIDTH - 1, batch, LRU_CB), F32),
            pltpu.VMEM((2, LRU_TS, batch, LRU_CB), F32),
            pltpu.VMEM((2, LRU_TS, batch, LRU_CB), F32),
            pltpu.VMEM((2, LRU_TS, batch, LRU_CB), F32),
            pltpu.VMEM((2, batch, LRU_CB), F32),
        ],
        compiler_params=_cparams(("parallel", "arbitrary")),
        name="lru",
    )(xr3, xr3, xr3, xr3, xr3, xr3, conv_w, conv_b, wgate, b_r, b_i, lam)


def _attn_kernel(q_ref, k_ref, v_ref, bias_ref, out_ref, acc_o, acc_l):
    g = pl.program_id(1)
    seq = q_ref.shape[1]
    nblk = ATT_OUT // LANES
    lane_head = lax.broadcasted_iota(jnp.int32, (1, ATT_OUT), 1) // HEAD_DIM

    def rows(start, n, d):
        return pl.ds(start, n) if d == 1 else pl.ds(start, n, stride=d)

    def load(ref, start, n, d):
        return jnp.concatenate([ref[j, rows(start, n, d), :] for j in range(nblk)], axis=1)

    def store(ref, start, n, d, val):
        for j in range(nblk):
            ref[j, rows(start, n, d), :] = val[:, j * LANES:(j + 1) * LANES]

    def run_group(gi):
        d = DILATIONS[gi]
        length = seq // d
        chunks_per_residue = length // ATT_CQ

        def chunk(n, _):
            r = n // chunks_per_residue
            qs = (n - r * chunks_per_residue) * ATT_CQ
            ks = jnp.clip(qs - HALF_KEYS, 0, length - ATT_KW)
            q0, k0 = r + d * qs, r + d * ks
            if d == 1:
                q0, k0 = pl.multiple_of(q0, ATT_CQ), pl.multiple_of(k0, HALF_KEYS)
            q = load(q_ref, q0, ATT_CQ, d).astype(BF16)
            kk = load(k_ref, k0, ATT_KW, d).astype(BF16)
            vv = load(v_ref, k0, ATT_KW, d).astype(BF16)
            q4 = jnp.concatenate([jnp.where(lane_head == h, q, jnp.zeros_like(q))
                                  for h in range(HEADS_PER_GROUP)], axis=0)
            s = lax.dot_general(q4, kk, (((1,), (1,)), ((), ())), preferred_element_type=F32)
            s = s + bias_ref[(qs - ks) // HALF_KEYS]
            m = jnp.max(s, axis=1, keepdims=True)
            p = jnp.exp2(s - m)
            l = jnp.sum(p, axis=1, keepdims=True)
            pv = jnp.dot(p.astype(BF16), vv, preferred_element_type=F32)
            inv = 1.0 / l
            lse = m * LN2 + jnp.log(l)
            o = jnp.zeros((ATT_CQ, ATT_OUT), F32)
            lf = jnp.zeros((ATT_CQ, ATT_OUT), F32)
            for h in range(HEADS_PER_GROUP):
                sl = slice(h * ATT_CQ, (h + 1) * ATT_CQ)
                sel = lane_head == h
                o = jnp.where(sel, pv[sl] * inv[sl], o)
                lf = jnp.where(sel, lse[sl], lf)
            if gi == 0:
                store(acc_o, q0, ATT_CQ, d, o)
                store(acc_l, q0, ATT_CQ, d, lf)
            else:
                ao = load(acc_o, q0, ATT_CQ, d)
                al = load(acc_l, q0, ATT_CQ, d)
                top = jnp.maximum(al, lf)
                wa, wg = jnp.exp(al - top), jnp.exp(lf - top)
                den = wa + wg
                store(acc_o, q0, ATT_CQ, d, (wa * ao + wg * o) * (1.0 / den))
                if gi < N_GROUPS - 1:
                    store(acc_l, q0, ATT_CQ, d, top + jnp.log(den))
            return 0

        lax.fori_loop(0, seq // ATT_CQ, chunk, 0, unroll=2)

    for gi in range(N_GROUPS):
        pl.when(g == gi)(functools.partial(run_group, gi))

    @pl.when(g == N_GROUPS - 1)
    def _():
        out_ref[...] = jnp.concatenate([acc_o[j] for j in range(nblk)], axis=1).astype(out_ref.dtype)


def _attention(qkv4, batch, seq):
    per_kind = ATT_WIDTH // ATT_OUT
    nblk = ATT_OUT // LANES

    def spec(kind):
        return pl.BlockSpec((None, nblk, seq, LANES), lambda b, g: (b, kind * per_kind + g, 0, 0))

    rows4 = HEADS_PER_GROUP * ATT_CQ
    qrow = (np.arange(rows4) % ATT_CQ)[None, :, None]
    kcol = np.arange(ATT_KW)[None, None, :]
    shift = (np.arange(3) * HALF_KEYS)[:, None, None]
    bias = jnp.asarray(np.where(np.abs(kcol - shift - qrow) <= HALF_KEYS, 0.0, -np.inf), F32)

    return pl.pallas_call(
        _attn_kernel,
        grid=(batch, N_GROUPS),
        in_specs=[spec(0), spec(1), spec(2), _resident((3, rows4, ATT_KW), lambda b, g: (0, 0, 0))],
        out_specs=pl.BlockSpec((seq, ATT_OUT), lambda b, g: (b, 0)),
        out_shape=jax.ShapeDtypeStruct((batch * seq, ATT_OUT), BF16),
        scratch_shapes=[pltpu.VMEM((nblk, seq, LANES), F32), pltpu.VMEM((nblk, seq, LANES), F32)],
        compiler_params=_cparams(("parallel", "arbitrary")),
        name="attention",
    )(qkv4, qkv4, qkv4, bias)


def _merge_kernel(x_ref, hf_ref, hb_ref, yg_ref, att_ref, wpl_ref, wpa_ref, wo_ref, out_ref):
    y = yg_ref[:, :D_MODEL].astype(F32)
    gelu = 0.5 * y * (1.0 + jnp.tanh(np.sqrt(2.0 / np.pi) * (y + 0.044715 * (y * y * y))))
    lru = ((hf_ref[...].astype(F32) + hb_ref[...].astype(F32)) * gelu).astype(BF16)
    branch_lru = jnp.dot(lru, wpl_ref[...], preferred_element_type=F32)
    branch_att = jnp.dot(att_ref[...], wpa_ref[...], preferred_element_type=F32)

    g_lru = _sigmoid(yg_ref[:, D_MODEL:2 * D_MODEL].astype(F32))
    g_att = _sigmoid(yg_ref[:, 2 * D_MODEL:].astype(F32))
    merged = (g_lru * branch_lru + g_att * branch_att).astype(BF16)
    out_ref[...] = x_ref[...] + jnp.dot(merged, wo_ref[...], preferred_element_type=F32)


def _merge(x2, hf2, hb2, yg, att, wpl, wpa, wo):
    tokens = x2.shape[0]
    tok = lambda w: pl.BlockSpec((TM, w), lambda i: (i, 0))
    return pl.pallas_call(
        _merge_kernel,
        grid=(tokens // TM,),
        in_specs=[tok(D_MODEL), tok(D_RNN), tok(D_RNN), tok(YG_WIDTH), tok(ATT_OUT)] + [
            _resident((D_RNN, D_MODEL), lambda i: (0, 0)),
            _resident((ATT_OUT, D_MODEL), lambda i: (0, 0)),
            _resident((D_MODEL, D_MODEL), lambda i: (0, 0)),
        ],
        out_specs=tok(D_MODEL),
        out_shape=jax.ShapeDtypeStruct((tokens, D_MODEL), F32),
        compiler_params=_cparams(("parallel",)),
        name="merge",
    )(x2, hf2, hb2, yg, att, wpl, wpa, wo)


def _ffn_chunks(d_ff):
    tiles = d_ff // 256
    sizes = [tiles // 3 + (1 if k < tiles % 3 else 0) for k in range(3)]
    out, start = [], 0
    for s in sizes:
        if s:
            out.append((start * 256, s * 256))
            start += s
    return out


def _ffn_kernel(x_ref, g_ref, wg_ref, wu_ref, wd_ref, out_ref):
    x = x_ref[...]
    hn = _rms(x, g_ref[...]).astype(BF16)
    acc = x
    for c0, width in _ffn_chunks(wg_ref.shape[1]):
        gate = jnp.dot(hn, wg_ref[:, c0:c0 + width], preferred_element_type=F32)
        up = jnp.dot(hn, wu_ref[:, c0:c0 + width], preferred_element_type=F32)
        act = (gate * _sigmoid(gate) * up).astype(BF16)
        acc = acc + jnp.dot(act, wd_ref[c0:c0 + width, :], preferred_element_type=F32)
    out_ref[...] = acc


def _ffn(x2, g, wg, wu, wd):
    tokens = x2.shape[0]
    d_ff = wg.shape[1]
    return pl.pallas_call(
        _ffn_kernel,
        grid=(tokens // TM,),
        in_specs=[
            pl.BlockSpec((TM, D_MODEL), lambda i: (i, 0)),
            _resident((1, D_MODEL), lambda i: (0, 0)),
            _resident((D_MODEL, d_ff), lambda i: (0, 0)),
            _resident((D_MODEL, d_ff), lambda i: (0, 0)),
            _resident((d_ff, D_MODEL), lambda i: (0, 0)),
        ],
        out_specs=pl.BlockSpec((TM, D_MODEL), lambda i: (i, 0)),
        out_shape=jax.ShapeDtypeStruct((tokens, D_MODEL), F32),
        compiler_params=_cparams(("parallel",)),
        name="ffn_dense",
    )(x2, g, wg, wu, wd)


def _router_kernel(x_ref, g_ref, whi_ref, wlo_ref, tri_ref, hp_ref, wc_ref, meta_ref, cnt_ref, carry_scr):
    i = pl.program_id(0)

    @pl.when(i == 0)
    def _():
        carry_scr[...] = jnp.zeros_like(carry_scr)

    hn = _rms(x_ref[...], g_ref[...])
    hi = hn.astype(BF16)
    lo = (hn - hi.astype(F32)).astype(BF16)
    hp_ref[...] = hn

    whi = whi_ref[...]
    logits = (jnp.dot(hi, whi, preferred_element_type=F32) + jnp.dot(lo, whi, preferred_element_type=F32)
              + jnp.dot(hi, wlo_ref[...], preferred_element_type=F32))
    tm = logits.shape[0]
    lane = lax.broadcasted_iota(jnp.int32, (tm, LANES), 1).astype(F32)
    lg = jnp.where(lane < N_EXPERTS, logits, -jnp.inf)
    m1 = jnp.max(lg, axis=1, keepdims=True)
    i1 = jnp.min(jnp.where(lg == m1, lane, float(LANES)), axis=1, keepdims=True)
    lg2 = jnp.where(lane == i1, -jnp.inf, lg)
    m2 = jnp.max(lg2, axis=1, keepdims=True)
    i2 = jnp.min(jnp.where(lg2 == m2, lane, float(LANES)), axis=1, keepdims=True)
    e = jnp.exp(m2 - m1)
    w1 = 1.0 / (1.0 + e)
    w2 = e * w1
    wc_ref[...] = jnp.where(lane == 0.0, w1, jnp.where(lane == 1.0, w2, 0.0))

    hot1 = lane == i1
    hot2 = lane == i2
    onehot = jnp.where(hot1 | hot2, 1.0, 0.0).astype(BF16)
    cum = jnp.dot(tri_ref[...], onehot, preferred_element_type=F32)
    tot = carry_scr[...] + cum
    r1 = jnp.sum(jnp.where(hot1, tot - 1.0, 0.0), axis=1, keepdims=True)
    r2 = jnp.sum(jnp.where(hot2, tot - 1.0, 0.0), axis=1, keepdims=True)
    carry_scr[...] = tot[tm - 1:tm, :]
    cnt_ref[...] = tot[tm - 1:tm, :].astype(jnp.int32)

    cols = jnp.where(lane == 0.0, i1, jnp.where(lane == 1.0, i2, jnp.where(lane == 2.0, r1,
                     jnp.where(lane == 3.0, r2, 0.0))))
    meta_ref[...] = cols.T[:SUBLANES, :].astype(jnp.int32)


def _router(x2, g, whi, wlo):
    tokens = x2.shape[0]
    tri = jnp.tril(jnp.ones((TM, TM), F32)).astype(BF16)
    return pl.pallas_call(
        _router_kernel,
        grid=(tokens // TM,),
        in_specs=[
            pl.BlockSpec((TM, D_MODEL), lambda i: (i, 0)),
            _resident((1, D_MODEL), lambda i: (0, 0)),
            _resident((D_MODEL, LANES), lambda i: (0, 0)),
            _resident((D_MODEL, LANES), lambda i: (0, 0)),
            _resident((TM, TM), lambda i: (0, 0)),
        ],
        out_specs=[
            pl.BlockSpec((TM, D_MODEL), lambda i: (i, 0)),
            pl.BlockSpec((TM, LANES), lambda i: (i, 0)),
            pl.BlockSpec((SUBLANES, TM), lambda i: (0, i)),
            pl.BlockSpec((1, LANES), lambda i: (0, 0)),
        ],
        out_shape=[
            jax.ShapeDtypeStruct((tokens, D_MODEL), F32),
            jax.ShapeDtypeStruct((tokens, LANES), F32),
            jax.ShapeDtypeStruct((SUBLANES, tokens), jnp.int32),
            jax.ShapeDtypeStruct((1, LANES), jnp.int32),
        ],
        scratch_shapes=[pltpu.VMEM((1, LANES), F32)],
        compiler_params=_cparams(("arbitrary",)),
        name="router",
    )(x2, g, whi, wlo, tri)


def _dispatch_kernel(pos_ref, prev_ref, hp_ref, xs_in_ref, xs_ref, sems):
    del xs_in_ref
    i = pl.program_id(0)
    n = pl.num_programs(0)
    tm = pos_ref.shape[1]

    def copy(tile, idx_ref, r, k):
        return pltpu.make_async_copy(hp_ref.at[pl.ds(tile * tm + r, 1)], xs_ref.at[pl.ds(idx_ref[k, r], 1)],
                                     sems.at[tile % 2])

    def start(r, _):
        copy(i, pos_ref, r, 0).start()
        copy(i, pos_ref, r, 1).start()
        return 0

    def wait_tile(tile, idx_ref):
        def wait(r, _):
            copy(tile, idx_ref, r, 0).wait()
            copy(tile, idx_ref, r, 1).wait()
            return 0
        lax.fori_loop(0, tm, wait, 0, unroll=8)

    lax.fori_loop(0, tm, start, 0, unroll=8)

    @pl.when(i > 0)
    def _():
        wait_tile(i - 1, prev_ref)

    @pl.when(i == n - 1)
    def _():
        wait_tile(i, pos_ref)


def _dispatch(pos3, hp, rows):
    tokens = hp.shape[0]
    xs0 = jnp.zeros((rows, D_MODEL), F32)
    return pl.pallas_call(
        _dispatch_kernel,
        grid=(tokens // GATHER_TM,),
        in_specs=[
            pl.BlockSpec((None, TOP_K, GATHER_TM), lambda i: (i, 0, 0), memory_space=pltpu.SMEM),
            pl.BlockSpec((None, TOP_K, GATHER_TM), lambda i: (jnp.maximum(i - 1, 0), 0, 0),
                         memory_space=pltpu.SMEM),
            pl.BlockSpec(memory_space=pl.ANY),
            pl.BlockSpec(memory_space=pl.ANY),
        ],
        out_specs=pl.BlockSpec(memory_space=pl.ANY),
        out_shape=jax.ShapeDtypeStruct((rows, D_MODEL), F32),
        scratch_shapes=[pltpu.SemaphoreType.DMA((2,))],
        input_output_aliases={3: 0},
        compiler_params=_cparams(("arbitrary",)),
        name="dispatch",
    )(pos3, pos3, hp, xs0)


def _moe_kernel(te_ref, tv_ref, xs_ref, wg_ref, wu_ref, wd_ref, ys_ref):
    i = pl.program_id(0)

    @pl.when(tv_ref[i] == 0)
    def _():
        ys_ref[...] = jnp.zeros_like(ys_ref)

    @pl.when(tv_ref[i] != 0)
    def _():
        x = xs_ref[...].astype(BF16)
        acc = jnp.zeros(ys_ref.shape, F32)
        for c in range(wg_ref.shape[1] // MOE_FC):
            cs = slice(c * MOE_FC, (c + 1) * MOE_FC)
            gate = jnp.dot(x, wg_ref[:, cs], preferred_element_type=F32)
            up = jnp.dot(x, wu_ref[:, cs], preferred_element_type=F32)
            act = (gate * _sigmoid(gate) * up).astype(BF16)
            acc = acc + jnp.dot(act, wd_ref[cs, :], preferred_element_type=F32)
        ys_ref[...] = acc


def _moe(tile_expert, tile_valid, xs, wg, wu, wd):
    rows = xs.shape[0]
    d_exp = wg.shape[2]
    grid_spec = pltpu.PrefetchScalarGridSpec(
        num_scalar_prefetch=2,
        grid=(rows // MOE_TM,),
        in_specs=[
            pl.BlockSpec((MOE_TM, D_MODEL), lambda i, te, tv: (i, 0)),
            pl.BlockSpec((None, D_MODEL, d_exp), lambda i, te, tv: (te[i], 0, 0), pipeline_mode=pl.Buffered(1)),
            pl.BlockSpec((None, D_MODEL, d_exp), lambda i, te, tv: (te[i], 0, 0), pipeline_mode=pl.Buffered(1)),
            pl.BlockSpec((None, d_exp, D_MODEL), lambda i, te, tv: (te[i], 0, 0), pipeline_mode=pl.Buffered(1)),
        ],
        out_specs=pl.BlockSpec((MOE_TM, D_MODEL), lambda i, te, tv: (i, 0)),
    )
    return pl.pallas_call(
        _moe_kernel,
        grid_spec=grid_spec,
        out_shape=jax.ShapeDtypeStruct((rows, D_MODEL), F32),
        compiler_params=_cparams(("arbitrary",)),
        name="experts",
    )(tile_expert, tile_valid, xs, wg, wu, wd)


def _combine_kernel(pos_ref, next_ref, x_ref, wc_ref, g_ref, ys_ref, out_ref, buf, sems):
    i = pl.program_id(0)
    n = pl.num_programs(0)
    tm = x_ref.shape[0]

    def copy(slot, idx_ref, r, k):
        return pltpu.make_async_copy(ys_ref.at[pl.ds(idx_ref[k, r], 1)], buf.at[slot, k, pl.ds(r, 1)],
                                     sems.at[slot])

    def start_tile(slot, idx_ref):
        def start(r, _):
            copy(slot, idx_ref, r, 0).start()
            copy(slot, idx_ref, r, 1).start()
            return 0
        lax.fori_loop(0, tm, start, 0, unroll=8)

    slot = i % 2

    @pl.when(i == 0)
    def _():
        start_tile(0, pos_ref)

    @pl.when(i + 1 < n)
    def _():
        start_tile(1 - slot, next_ref)

    def wait(r, _):
        copy(slot, pos_ref, r, 0).wait()
        copy(slot, pos_ref, r, 1).wait()
        return 0

    lax.fori_loop(0, tm, wait, 0, unroll=8)
    wc = wc_ref[...]
    y = x_ref[...] + (wc[:, 0:1] * buf[slot, 0] + wc[:, 1:2] * buf[slot, 1])
    out_ref[...] = _rms(y, g_ref[...])


def _combine(pos3, x2, wcols, g, ys):
    tokens = x2.shape[0]
    ntiles = tokens // GATHER_TM
    return pl.pallas_call(
        _combine_kernel,
        grid=(tokens // GATHER_TM,),
        in_specs=[
            pl.BlockSpec((None, TOP_K, GATHER_TM), lambda i: (i, 0, 0), memory_space=pltpu.SMEM),
            pl.BlockSpec((None, TOP_K, GATHER_TM), lambda i: (jnp.minimum(i + 1, ntiles - 1), 0, 0),
                         memory_space=pltpu.SMEM),
            pl.BlockSpec((GATHER_TM, D_MODEL), lambda i: (i, 0)),
            pl.BlockSpec((GATHER_TM, LANES), lambda i: (i, 0)),
            _resident((1, D_MODEL), lambda i: (0, 0)),
            pl.BlockSpec(memory_space=pl.ANY),
        ],
        out_specs=pl.BlockSpec((GATHER_TM, D_MODEL), lambda i: (i, 0)),
        out_shape=jax.ShapeDtypeStruct((tokens, D_MODEL), F32),
        scratch_shapes=[pltpu.VMEM((2, TOP_K, GATHER_TM, D_MODEL), F32), pltpu.SemaphoreType.DMA((2,))],
        compiler_params=_cparams(("arbitrary",)),
        name="combine",
    )(pos3, pos3, x2, wcols, g, ys)


def _rotary_lane_tables(seq):
    half = ROT_DIM // 2
    pos = jnp.arange(seq, dtype=F32)
    inv_freq = ROPE_THETA ** (-jnp.arange(0, ROT_DIM, 2, dtype=F32) / ROT_DIM)
    ang = pos[:, None] * inv_freq[None, :]
    cos, sin = jnp.cos(ang), jnp.sin(ang)
    j = np.arange(LANES) % HEAD_DIM
    idx = j % half
    cosf = jnp.where((j < ROT_DIM)[None, :], cos[:, idx], 1.0)
    sina = jnp.where(((j >= half) & (j < ROT_DIM))[None, :], sin[:, idx], 0.0)
    sinb = jnp.where((j < half)[None, :], -sin[:, idx], 0.0)
    return cosf, sina, sinb


def _mixer(x2, layer, p, tables, batch, seq):
    cuts = np.cumsum([D_RNN, D_RNN, ATT_WIDTH, ATT_WIDTH, ATT_WIDTH])
    w = p["w_in"][layer]
    w_perm = jnp.concatenate([w[:, :cuts[1]], w[:, cuts[4]:], w[:, cuts[1]:cuts[4]]], axis=1).astype(BF16)
    xr, yg, qkv4 = _in_proj(x2, p["norm_mix_g"][layer][None], w_perm, *tables, batch, seq)

    wgate = (0.5 * jnp.concatenate([p["w_rgate"][layer], p["w_igate"][layer]], axis=-1)).astype(BF16)
    hf, hb = _lru(xr.reshape(batch, seq, D_RNN), p["conv_w"][layer], p["conv_b"][layer][None], wgate,
                  0.5 * p["b_rgate"][layer], 0.5 * p["b_igate"][layer], p["lru_lambda"][layer])
    att = _attention(qkv4, batch, seq)

    tokens = batch * seq
    return _merge(x2, hf.reshape(tokens, D_RNN), hb.reshape(tokens, D_RNN), yg, att,
                  p["w_proj_lru"][layer].astype(BF16), p["w_proj_att"][layer].astype(BF16),
                  p["w_out"][layer].astype(BF16))


def _moe_layer(x2, j, p, final_g):
    tokens = x2.shape[0]
    wr = jnp.pad(p["w_router"][j], ((0, 0), (0, LANES - N_EXPERTS)))
    whi = wr.astype(BF16)
    wlo = (wr - whi.astype(F32)).astype(BF16)
    hp, wcols, meta, counts = _router(x2, p["norm_ffn_g"][j * 2 + 1][None], whi, wlo)

    counts = counts[0, :N_EXPERTS]
    padded = ((counts + MOE_TM - 1) // MOE_TM) * MOE_TM
    ends = jnp.cumsum(padded)
    starts = ends - padded
    experts_of = meta[0:TOP_K]
    base = jnp.zeros_like(experts_of)
    for e in range(N_EXPERTS):
        base = base + jnp.where(experts_of == e, starts[e], 0)
    pos = base + meta[TOP_K:2 * TOP_K]
    pos3 = pos.reshape(TOP_K, tokens // GATHER_TM, GATHER_TM).transpose(1, 0, 2)

    rows = TOP_K * tokens + N_EXPERTS * MOE_TM
    tile_start = jnp.arange(rows // MOE_TM, dtype=jnp.int32) * MOE_TM
    tile_expert = jnp.minimum(jnp.sum(tile_start[:, None] >= ends[None, :], axis=1), N_EXPERTS - 1).astype(jnp.int32)
    tile_valid = (tile_start < ends[-1]).astype(jnp.int32)

    xs = _dispatch(pos3, hp, rows)
    ys = _moe(tile_expert, tile_valid, xs, p["w_exp_gate"][j].astype(BF16), p["w_exp_up"][j].astype(BF16),
              p["w_exp_down"][j].astype(BF16))
    return _combine(pos3, x2, wcols, final_g[None], ys)


def kernel(x, norm_mix_g, w_in, conv_w, conv_b, w_rgate, b_rgate, w_igate, b_igate, lru_lambda, w_proj_lru,
           w_proj_att, w_out, norm_ffn_g, w_dense_gate, w_dense_up, w_dense_down, w_router, w_exp_gate, w_exp_up,
           w_exp_down, final_norm_g):
    batch, seq, _ = x.shape
    depth = w_in.shape[0]
    assert depth % 2 == 0, "the final RMSNorm is fused into the last (routed) layer"
    p = dict(norm_mix_g=norm_mix_g, w_in=w_in, conv_w=conv_w, conv_b=conv_b, w_rgate=w_rgate, b_rgate=b_rgate,
             w_igate=w_igate, b_igate=b_igate, lru_lambda=lru_lambda, w_proj_lru=w_proj_lru,
             w_proj_att=w_proj_att, w_out=w_out, norm_ffn_g=norm_ffn_g, w_router=w_router,
             w_exp_gate=w_exp_gate, w_exp_up=w_exp_up, w_exp_down=w_exp_down)
    tables = _rotary_lane_tables(seq)
    x2 = x.reshape(batch * seq, D_MODEL)
    for layer in range(depth):
        x2 = _mixer(x2, layer, p, tables, batch, seq)
        j = layer // 2
        if layer % 2 == 0:
            x2 = _ffn(x2, norm_ffn_g[layer][None], w_dense_gate[j].astype(BF16), w_dense_up[j].astype(BF16),
                      w_dense_down[j].astype(BF16))
        else:
            assert layer == depth - 1
            x2 = _moe_layer(x2, j, p, final_norm_g)
    return x2.reshape(batch, seq, D_MODEL)
```

```python
import functools

import jax
import jax.numpy as jnp
import numpy as np
from jax import lax
from jax.experimental import pallas as pl
from jax.experimental.pallas import tpu as pltpu

D_MODEL = 1024
D_RNN = D_MODEL
N_LRU_BLOCKS = 8
LRU_BLOCK = D_RNN // N_LRU_BLOCKS
CONV_WIDTH = 4
LRU_C = 8.0
HEAD_DIM = 64
HEADS_PER_GROUP = 4
WINDOWS = (128, 512, 2048)
DILATIONS = (1, 4, 16)
N_GROUPS = len(WINDOWS)
ATT_WIDTH = N_GROUPS * HEADS_PER_GROUP * HEAD_DIM
ATT_OUT = HEADS_PER_GROUP * HEAD_DIM
ROT_DIM = HEAD_DIM // 4
ROPE_THETA = 500000.0
HALF_KEYS = 64
IN_WIDTH = 2 * D_RNN + 3 * ATT_WIDTH + 2 * D_MODEL
QKV_WIDTH = 3 * ATT_WIDTH
YG_WIDTH = 3 * D_MODEL
N_EXPERTS = 8
TOP_K = 2
RMS_EPS = 1e-6

LANES = 128
SUBLANES = 8
VMEM_LIMIT = 52 * 1024 * 1024

TM = 512
LRU_TS = 256
LRU_CB = 256
ATT_CQ = 128
ATT_KW = ATT_CQ + 2 * HALF_KEYS
MOE_TM = 512
MOE_FC = 512
GATHER_TM = 256

BF16 = jnp.bfloat16
F32 = jnp.float32
LOG2E = float(np.log2(np.e))
LN2 = float(np.log(2.0))
Q_SCALE = HEAD_DIM ** -0.5 * LOG2E
SQRT_FLOOR = 1e-30


def _cparams(sem):
    return pltpu.CompilerParams(dimension_semantics=sem, vmem_limit_bytes=VMEM_LIMIT)


def _resident(shape, index_map):
    return pl.BlockSpec(shape, index_map, pipeline_mode=pl.Buffered(1))


def _sigmoid(x):
    return 0.5 * jnp.tanh(0.5 * x) + 0.5


def _rms(x, g):
    ms = jnp.mean(x * x, axis=-1, keepdims=True)
    return x * lax.rsqrt(ms + RMS_EPS) * g


def _in_proj_kernel(x_ref, g_ref, w_ref, cos_ref, sa_ref, sb_ref, xr_ref, yg_ref, qkv_ref):
    hn = _rms(x_ref[...], g_ref[...]).astype(BF16)

    def proj(c0, width):
        return jnp.dot(hn, w_ref[:, c0:c0 + width], preferred_element_type=F32)

    for c in range(D_RNN // 512):
        xr_ref[:, c * 512:(c + 1) * 512] = proj(c * 512, 512)
    for c in range(YG_WIDTH // 512):
        yg_ref[:, c * 512:(c + 1) * 512] = proj(D_RNN + c * 512, 512).astype(BF16)

    cosf, sina, sinb = cos_ref[...], sa_ref[...], sb_ref[...]
    base = D_RNN + YG_WIDTH
    for c in range(QKV_WIDTH // 256):
        acc = proj(base + c * 256, 256)
        if c < 2 * ATT_WIDTH // 256:
            for h in range(2):
                t = acc[:, h * LANES:(h + 1) * LANES]
                r = t * cosf + pltpu.roll(t, 8, 1) * sina + pltpu.roll(t, LANES - 8, 1) * sinb
                qkv_ref[2 * c + h] = r * Q_SCALE if c < ATT_WIDTH // 256 else r
        else:
            for h in range(2):
                qkv_ref[2 * c + h] = acc[:, h * LANES:(h + 1) * LANES]


def _in_proj(x2, g, w_perm, cosf, sina, sinb, batch, seq):
    tokens = batch * seq
    ns = seq // TM
    return pl.pallas_call(
        _in_proj_kernel,
        grid=(tokens // TM,),
        in_specs=[
            pl.BlockSpec((TM, D_MODEL), lambda i: (i, 0)),
            _resident((1, D_MODEL), lambda i: (0, 0)),
            _resident((D_MODEL, IN_WIDTH), lambda i: (0, 0)),
            pl.BlockSpec((TM, LANES), lambda i: (i % ns, 0)),
            pl.BlockSpec((TM, LANES), lambda i: (i % ns, 0)),
            pl.BlockSpec((TM, LANES), lambda i: (i % ns, 0)),
        ],
        out_specs=[
            pl.BlockSpec((TM, D_RNN), lambda i: (i, 0)),
            pl.BlockSpec((TM, YG_WIDTH), lambda i: (i, 0)),
            pl.BlockSpec((None, QKV_WIDTH // LANES, TM, LANES), lambda i: (i // ns, 0, i % ns, 0)),
        ],
        out_shape=[
            jax.ShapeDtypeStruct((tokens, D_RNN), F32),
            jax.ShapeDtypeStruct((tokens, YG_WIDTH), BF16),
            jax.ShapeDtypeStruct((batch, QKV_WIDTH // LANES, seq, LANES), F32),
        ],
        compiler_params=_cparams(("parallel",)),
        name="in_proj",
    )(x2, g, w_perm, cosf, sina, sinb)


def _lru_kernel(xf_ref, xfp_ref, xfn_ref, xb_ref, xbp_ref, xbn_ref, cw_ref, cb_ref, wg_ref,
                br_ref, bi_ref, lam_ref, hf_ref, hb_ref, xp_scr, a_scr, u_scr, h_scr, carry_scr):
    c = pl.program_id(1)
    nc = pl.num_programs(1)
    nb = xf_ref.shape[0]
    ts = xf_ref.shape[1]

    @pl.when(c == 0)
    def _():
        carry_scr[...] = jnp.zeros_like(carry_scr)

    def time_major(v):
        return jnp.swapaxes(v, 0, 1)

    def gates(direction, x_ref, prev_ref, next_ref, chunk):
        halo_p = time_major(prev_ref[...])
        halo_n = time_major(next_ref[...])
        xp_scr[0] = jnp.where(chunk > 0, halo_p[SUBLANES - 1], 0.0)
        xp_scr[1:ts + 1] = time_major(x_ref[...])
        xp_scr[ts + 1] = jnp.where(chunk < nc - 1, halo_n[0], 0.0)
        xp_scr[ts + 2] = jnp.where(chunk < nc - 1, halo_n[1], 0.0)
        cw = cw_ref[...]
        xc = cb_ref[...][None]
        for j in range(CONV_WIDTH):
            xc = xc + xp_scr[j:j + ts] * cw[j:j + 1][None]
        for j in range(LRU_CB // LRU_BLOCK):
            sl = slice(j * LRU_BLOCK, (j + 1) * LRU_BLOCK)
            xcb = xc[:, :, sl].reshape(ts * nb, LRU_BLOCK)
            rg = jnp.dot(xcb.astype(BF16), wg_ref[direction, j], preferred_element_type=F32)
            tr = jnp.tanh(rg[:, :LRU_BLOCK] + br_ref[direction:direction + 1, sl])
            ti = jnp.tanh(rg[:, LRU_BLOCK:] + bi_ref[direction:direction + 1, sl])
            lam = lam_ref[direction:direction + 1, sl]
            c2 = (-0.5 * LRU_C * LOG2E) * (jnp.maximum(-lam, 0.0) + jnp.log(1.0 + jnp.exp(-jnp.abs(lam))))
            a = jnp.exp2(tr * c2 + c2)
            xh = 0.5 * xcb
            gated = ti * xh + xh
            om = 1.0 - a * a
            u = (om * lax.rsqrt(jnp.maximum(om, SQRT_FLOOR))) * gated
            a_scr[direction, :, :, sl] = a.reshape(ts, nb, LRU_BLOCK)
            u_scr[direction, :, :, sl] = u.reshape(ts, nb, LRU_BLOCK)

    gates(0, xf_ref, xfp_ref, xfn_ref, c)
    gates(1, xb_ref, xbp_ref, xbn_ref, nc - 1 - c)

    def step(t, carry):
        hf, hb = carry
        tb = ts - 1 - t
        hf = a_scr[0, t] * hf + u_scr[0, t]
        hb = a_scr[1, tb] * hb + u_scr[1, tb]
        h_scr[0, t] = hf
        h_scr[1, tb] = hb
        return hf, hb

    hf, hb = lax.fori_loop(0, ts, step, (carry_scr[0], carry_scr[1]), unroll=8)
    carry_scr[0] = hf
    carry_scr[1] = hb
    hf_ref[...] = jnp.swapaxes(h_scr[0], 0, 1).astype(hf_ref.dtype)
    hb_ref[...] = jnp.swapaxes(h_scr[1], 0, 1).astype(hb_ref.dtype)


def _lru(xr3, conv_w, conv_b, wgate, b_r, b_i, lam):
    batch, seq, _ = xr3.shape
    nc = seq // LRU_TS
    ncb = D_RNN // LRU_CB
    hblk = LRU_TS // SUBLANES
    nh = seq // SUBLANES

    def cur(rev):
        return lambda n, c: (0, (nc - 1 - c) if rev else c, n)

    def prev(rev):
        return lambda n, c: (0, jnp.maximum(((nc - 1 - c) if rev else c) * hblk - 1, 0), n)

    def nxt(rev):
        return lambda n, c: (0, jnp.minimum((((nc - 1 - c) if rev else c) + 1) * hblk, nh - 1), n)

    main = lambda rev: pl.BlockSpec((batch, LRU_TS, LRU_CB), cur(rev))
    halo_p = lambda rev: pl.BlockSpec((batch, SUBLANES, LRU_CB), prev(rev))
    halo_n = lambda rev: pl.BlockSpec((batch, SUBLANES, LRU_CB), nxt(rev))
    vec2 = pl.BlockSpec((2, LRU_CB), lambda n, c: (0, n))
    out_sds = jax.ShapeDtypeStruct((batch, seq, D_RNN), BF16)
    return pl.pallas_call(
        _lru_kernel,
        grid=(ncb, nc),
        in_specs=[
            main(False), halo_p(False), halo_n(False),
            main(True), halo_p(True), halo_n(True),
            pl.BlockSpec((CONV_WIDTH, LRU_CB), lambda n, c: (0, n)),
            pl.BlockSpec((1, LRU_CB), lambda n, c: (0, n)),
            pl.BlockSpec((2, LRU_CB // LRU_BLOCK, LRU_BLOCK, 2 * LRU_BLOCK), lambda n, c: (0, n, 0, 0)),
            vec2, vec2, vec2,
        ],
        out_specs=[main(False), main(True)],
        out_shape=[out_sds, out_sds],
        scratch_shapes=[
            pltpu.VMEM((LRU_TS + CONV_WIDTH - 1, batch, LRU_CB), F32),
            pltpu.VMEM((2, LRU_TS, batch, LRU_CB), F32),
            pltpu.VMEM((2, LRU_TS, batch, LRU_CB), F32),
            pltpu.VMEM((2, LRU_TS, batch, LRU_CB), F32),
            pltpu.VMEM((2, batch, LRU_CB), F32),
        ],
        compiler_params=_cparams(("parallel", "arbitrary")),
        name="lru",
    )(xr3, xr3, xr3, xr3, xr3, xr3, conv_w, conv_b, wgate, b_r, b_i, lam)


def _attn_kernel(q_ref, k_ref, v_ref, bias_ref, out_ref, acc_o, acc_l):
    g = pl.program_id(1)
    seq = q_ref.shape[1]
    nblk = ATT_OUT // LANES
    lane_head = lax.broadcasted_iota(jnp.int32, (1, ATT_OUT), 1) // HEAD_DIM

    def rows(start, n, d):
        return pl.ds(start, n) if d == 1 else pl.ds(start, n, stride=d)

    def load(ref, start, n, d):
        return jnp.concatenate([ref[j, rows(start, n, d), :] for j in range(nblk)], axis=1)

    def store(ref, start, n, d, val):
        for j in range(nblk):
            ref[j, rows(start, n, d), :] = val[:, j * LANES:(j + 1) * LANES]

    def run_group(gi):
        d = DILATIONS[gi]
        length = seq // d
        chunks_per_residue = length // ATT_CQ

        def chunk(n, _):
            r = n // chunks_per_residue
            qs = (n - r * chunks_per_residue) * ATT_CQ
            ks = jnp.clip(qs - HALF_KEYS, 0, length - ATT_KW)
            q0, k0 = r + d * qs, r + d * ks
            if d == 1:
                q0, k0 = pl.multiple_of(q0, ATT_CQ), pl.multiple_of(k0, HALF_KEYS)
            q = load(q_ref, q0, ATT_CQ, d).astype(BF16)
            kk = load(k_ref, k0, ATT_KW, d).astype(BF16)
            vv = load(v_ref, k0, ATT_KW, d).astype(BF16)
            q4 = jnp.concatenate([jnp.where(lane_head == h, q, jnp.zeros_like(q))
                                  for h in range(HEADS_PER_GROUP)], axis=0)
            s = lax.dot_general(q4, kk, (((1,), (1,)), ((), ())), preferred_element_type=F32)
            s = s + bias_ref[(qs - ks) // HALF_KEYS]
            m = jnp.max(s, axis=1, keepdims=True)
            p = jnp.exp2(s - m)
            l = jnp.sum(p, axis=1, keepdims=True)
            pv = jnp.dot(p.astype(BF16), vv, preferred_element_type=F32)
            inv = 1.0 / l
            lse = m * LN2 + jnp.log(l)
            o = jnp.zeros((ATT_CQ, ATT_OUT), F32)
            lf = jnp.zeros((ATT_CQ, ATT_OUT), F32)
            for h in range(HEADS_PER_GROUP):
                sl = slice(h * ATT_CQ, (h + 1) * ATT_CQ)
                sel = lane_head == h
                o = jnp.where(sel, pv[sl] * inv[sl], o)
                lf = jnp.where(sel, lse[sl], lf)
            if gi == 0:
                store(acc_o, q0, ATT_CQ, d, o)
                store(acc_l, q0, ATT_CQ, d, lf)
            else:
                ao = load(acc_o, q0, ATT_CQ, d)
                al = load(acc_l, q0, ATT_CQ, d)
                top = jnp.maximum(al, lf)
                wa, wg = jnp.exp(al - top), jnp.exp(lf - top)
                den = wa + wg
                store(acc_o, q0, ATT_CQ, d, (wa * ao + wg * o) * (1.0 / den))
                if gi < N_GROUPS - 1:
                    store(acc_l, q0, ATT_CQ, d, top + jnp.log(den))
            return 0

        lax.fori_loop(0, seq // ATT_CQ, chunk, 0, unroll=2)

    for gi in range(N_GROUPS):
        pl.when(g == gi)(functools.partial(run_group, gi))

    @pl.when(g == N_GROUPS - 1)
    def _():
        out_ref[...] = jnp.concatenate([acc_o[j] for j in range(nblk)], axis=1).astype(out_ref.dtype)


def _attention(qkv4, batch, seq):
    per_kind = ATT_WIDTH // ATT_OUT
    nblk = ATT_OUT // LANES

    def spec(kind):
        return pl.BlockSpec((None, nblk, seq, LANES), lambda b, g: (b, kind * per_kind + g, 0, 0))

    rows4 = HEADS_PER_GROUP * ATT_CQ
    qrow = (np.arange(rows4) % ATT_CQ)[None, :, None]
    kcol = np.arange(ATT_KW)[None, None, :]
    shift = (np.arange(3) * HALF_KEYS)[:, None, None]
    bias = jnp.asarray(np.where(np.abs(kcol - shift - qrow) <= HALF_KEYS, 0.0, -np.inf), F32)

    return pl.pallas_call(
        _attn_kernel,
        grid=(batch, N_GROUPS),
        in_specs=[spec(0), spec(1), spec(2), _resident((3, rows4, ATT_KW), lambda b, g: (0, 0, 0))],
        out_specs=pl.BlockSpec((seq, ATT_OUT), lambda b, g: (b, 0)),
        out_shape=jax.ShapeDtypeStruct((batch * seq, ATT_OUT), BF16),
        scratch_shapes=[pltpu.VMEM((nblk, seq, LANES), F32), pltpu.VMEM((nblk, seq, LANES), F32)],
        compiler_params=_cparams(("parallel", "arbitrary")),
        name="attention",
    )(qkv4, qkv4, qkv4, bias)


def _merge_kernel(x_ref, hf_ref, hb_ref, yg_ref, att_ref, wpl_ref, wpa_ref, wo_ref, out_ref):
    y = yg_ref[:, :D_MODEL].astype(F32)
    gelu = 0.5 * y * (1.0 + jnp.tanh(np.sqrt(2.0 / np.pi) * (y + 0.044715 * (y * y * y))))
    lru = ((hf_ref[...].astype(F32) + hb_ref[...].astype(F32)) * gelu).astype(BF16)
    branch_lru = jnp.dot(lru, wpl_ref[...], preferred_element_type=F32)
    branch_att = jnp.dot(att_ref[...], wpa_ref[...], preferred_element_type=F32)

    g_lru = _sigmoid(yg_ref[:, D_MODEL:2 * D_MODEL].astype(F32))
    g_att = _sigmoid(yg_ref[:, 2 * D_MODEL:].astype(F32))
    merged = (g_lru * branch_lru + g_att * branch_att).astype(BF16)
    out_ref[...] = x_ref[...] + jnp.dot(merged, wo_ref[...], preferred_element_type=F32)


def _merge(x2, hf2, hb2, yg, att, wpl, wpa, wo):
    tokens = x2.shape[0]
    tok = lambda w: pl.BlockSpec((TM, w), lambda i: (i, 0))
    return pl.pallas_call(
        _merge_kernel,
        grid=(tokens // TM,),
        in_specs=[tok(D_MODEL), tok(D_RNN), tok(D_RNN), tok(YG_WIDTH), tok(ATT_OUT)] + [
            _resident((D_RNN, D_MODEL), lambda i: (0, 0)),
            _resident((ATT_OUT, D_MODEL), lambda i: (0, 0)),
            _resident((D_MODEL, D_MODEL), lambda i: (0, 0)),
        ],
        out_specs=tok(D_MODEL),
        out_shape=jax.ShapeDtypeStruct((tokens, D_MODEL), F32),
        compiler_params=_cparams(("parallel",)),
        name="merge",
    )(x2, hf2, hb2, yg, att, wpl, wpa, wo)


def _ffn_chunks(d_ff):
    tiles = d_ff // 256
    sizes = [tiles // 3 + (1 if k < tiles % 3 else 0) for k in range(3)]
    out, start = [], 0
    for s in sizes:
        if s:
            out.append((start * 256, s * 256))
            start += s
    return out


def _ffn_kernel(x_ref, g_ref, wg_ref, wu_ref, wd_ref, out_ref):
    x = x_ref[...]
    hn = _rms(x, g_ref[...]).astype(BF16)
    acc = x
    for c0, width in _ffn_chunks(wg_ref.shape[1]):
        gate = jnp.dot(hn, wg_ref[:, c0:c0 + width], preferred_element_type=F32)
        up = jnp.dot(hn, wu_ref[:, c0:c0 + width], preferred_element_type=F32)
        act = (gate * _sigmoid(gate) * up).astype(BF16)
        acc = acc + jnp.dot(act, wd_ref[c0:c0 + width, :], preferred_element_type=F32)
    out_ref[...] = acc


def _ffn(x2, g, wg, wu, wd):
    tokens = x2.shape[0]
    d_ff = wg.shape[1]
    return pl.pallas_call(
        _ffn_kernel,
        grid=(tokens // TM,),
        in_specs=[
            pl.BlockSpec((TM, D_MODEL), lambda i: (i, 0)),
            _resident((1, D_MODEL), lambda i: (0, 0)),
            _resident((D_MODEL, d_ff), lambda i: (0, 0)),
            _resident((D_MODEL, d_ff), lambda i: (0, 0)),
            _resident((d_ff, D_MODEL), lambda i: (0, 0)),
        ],
        out_specs=pl.BlockSpec((TM, D_MODEL), lambda i: (i, 0)),
        out_shape=jax.ShapeDtypeStruct((tokens, D_MODEL), F32),
        compiler_params=_cparams(("parallel",)),
        name="ffn_dense",
    )(x2, g, wg, wu, wd)


def _router_kernel(x_ref, g_ref, whi_ref, wlo_ref, tri_ref, hp_ref, wc_ref, meta_ref, cnt_ref, carry_scr):
    i = pl.program_id(0)

    @pl.when(i == 0)
    def _():
        carry_scr[...] = jnp.zeros_like(carry_scr)

    hn = _rms(x_ref[...], g_ref[...])
    hi = hn.astype(BF16)
    lo = (hn - hi.astype(F32)).astype(BF16)
    hp_ref[...] = hn

    whi = whi_ref[...]
    logits = (jnp.dot(hi, whi, preferred_element_type=F32) + jnp.dot(lo, whi, preferred_element_type=F32)
              + jnp.dot(hi, wlo_ref[...], preferred_element_type=F32))
    tm = logits.shape[0]
    lane = lax.broadcasted_iota(jnp.int32, (tm, LANES), 1).astype(F32)
    lg = jnp.where(lane < N_EXPERTS, logits, -jnp.inf)
    m1 = jnp.max(lg, axis=1, keepdims=True)
    i1 = jnp.min(jnp.where(lg == m1, lane, float(LANES)), axis=1, keepdims=True)
    lg2 = jnp.where(lane == i1, -jnp.inf, lg)
    m2 = jnp.max(lg2, axis=1, keepdims=True)
    i2 = jnp.min(jnp.where(lg2 == m2, lane, float(LANES)), axis=1, keepdims=True)
    e = jnp.exp(m2 - m1)
    w1 = 1.0 / (1.0 + e)
    w2 = e * w1
    wc_ref[...] = jnp.where(lane == 0.0, w1, jnp.where(lane == 1.0, w2, 0.0))

    hot1 = lane == i1
    hot2 = lane == i2
    onehot = jnp.where(hot1 | hot2, 1.0, 0.0).astype(BF16)
    cum = jnp.dot(tri_ref[...], onehot, preferred_element_type=F32)
    tot = carry_scr[...] + cum
    r1 = jnp.sum(jnp.where(hot1, tot - 1.0, 0.0), axis=1, keepdims=True)
    r2 = jnp.sum(jnp.where(hot2, tot - 1.0, 0.0), axis=1, keepdims=True)
    carry_scr[...] = tot[tm - 1:tm, :]
    cnt_ref[...] = tot[tm - 1:tm, :].astype(jnp.int32)

    cols = jnp.where(lane == 0.0, i1, jnp.where(lane == 1.0, i2, jnp.where(lane == 2.0, r1,
                     jnp.where(lane == 3.0, r2, 0.0))))
    meta_ref[...] = cols.T[:SUBLANES, :].astype(jnp.int32)


def _router(x2, g, whi, wlo):
    tokens = x2.shape[0]
    tri = jnp.tril(jnp.ones((TM, TM), F32)).astype(BF16)
    return pl.pallas_call(
        _router_kernel,
        grid=(tokens // TM,),
        in_specs=[
            pl.BlockSpec((TM, D_MODEL), lambda i: (i, 0)),
            _resident((1, D_MODEL), lambda i: (0, 0)),
            _resident((D_MODEL, LANES), lambda i: (0, 0)),
            _resident((D_MODEL, LANES), lambda i: (0, 0)),
            _resident((TM, TM), lambda i: (0, 0)),
        ],
        out_specs=[
            pl.BlockSpec((TM, D_MODEL), lambda i: (i, 0)),
            pl.BlockSpec((TM, LANES), lambda i: (i, 0)),
            pl.BlockSpec((SUBLANES, TM), lambda i: (0, i)),
            pl.BlockSpec((1, LANES), lambda i: (0, 0)),
        ],
        out_shape=[
            jax.ShapeDtypeStruct((tokens, D_MODEL), F32),
            jax.ShapeDtypeStruct((tokens, LANES), F32),
            jax.ShapeDtypeStruct((SUBLANES, tokens), jnp.int32),
            jax.ShapeDtypeStruct((1, LANES), jnp.int32),
        ],
        scratch_shapes=[pltpu.VMEM((1, LANES), F32)],
        compiler_params=_cparams(("arbitrary",)),
        name="router",
    )(x2, g, whi, wlo, tri)


def _dispatch_kernel(pad_ref, pos_ref, prev_ref, hp_ref, xs_ref, stage, zeros, sems, zsem):
    i = pl.program_id(0)
    n = pl.num_programs(0)
    tm = pos_ref.shape[1]

    @pl.when(i == 0)
    def _():
        zeros[...] = jnp.zeros_like(zeros)
        fill_rows = zeros.shape[0]
        used = pad_ref[N_EXPERTS]
        total = xs_ref.shape[0] - fill_rows

        def fill(start):
            return pltpu.make_async_copy(zeros, xs_ref.at[pl.ds(pl.multiple_of(start, SUBLANES), fill_rows)], zsem)

        def tail(t):
            return pltpu.make_async_copy(zeros.at[pl.ds(0, MOE_TM)], xs_ref.at[pl.ds(t * MOE_TM, MOE_TM)], zsem)

        def run(cp):
            cp.start()
            cp.wait()

        for e in range(N_EXPERTS):
            run(fill(pad_ref[e]))
        run(fill(total))
        for t in range(total // MOE_TM - N_EXPERTS, total // MOE_TM):
            pl.when(t * MOE_TM >= used)(functools.partial(run, tail(t)))

    def copy(tile, idx_ref, r, k):
        slot = tile % 2
        return pltpu.make_async_copy(stage.at[slot, pl.ds(r, 1)], xs_ref.at[pl.ds(idx_ref[k, r], 1)],
                                     sems.at[slot])

    stage[i % 2] = hp_ref[...]

    def start(r, _):
        copy(i, pos_ref, r, 0).start()
        copy(i, pos_ref, r, 1).start()
        return 0

    def wait_tile(tile, idx_ref):
        def wait(r, _):
            copy(tile, idx_ref, r, 0).wait()
            copy(tile, idx_ref, r, 1).wait()
            return 0
        lax.fori_loop(0, tm, wait, 0, unroll=8)

    lax.fori_loop(0, tm, start, 0, unroll=8)

    @pl.when(i > 0)
    def _():
        wait_tile(i - 1, prev_ref)

    @pl.when(i == n - 1)
    def _():
        wait_tile(i, pos_ref)


FILL_ROWS = MOE_TM + SUBLANES


def _dispatch(pad_start8, pos3, hp, rows):
    tokens = hp.shape[0]
    return pl.pallas_call(
        _dispatch_kernel,
        grid=(tokens // GATHER_TM,),
        in_specs=[
            pl.BlockSpec(memory_space=pltpu.SMEM),
            pl.BlockSpec((None, TOP_K, GATHER_TM), lambda i: (i, 0, 0), memory_space=pltpu.SMEM),
            pl.BlockSpec((None, TOP_K, GATHER_TM), lambda i: (jnp.maximum(i - 1, 0), 0, 0),
                         memory_space=pltpu.SMEM),
            pl.BlockSpec((GATHER_TM, D_MODEL), lambda i: (i, 0)),
        ],
        out_specs=pl.BlockSpec(memory_space=pl.ANY),
        out_shape=jax.ShapeDtypeStruct((rows + FILL_ROWS, D_MODEL), F32),
        scratch_shapes=[pltpu.VMEM((2, GATHER_TM, D_MODEL), F32), pltpu.VMEM((FILL_ROWS, D_MODEL), F32),
                        pltpu.SemaphoreType.DMA((2,)), pltpu.SemaphoreType.DMA],
        compiler_params=_cparams(("arbitrary",)),
        name="dispatch",
    )(pad_start8, pos3, pos3, hp)


def _moe_kernel(te_ref, tv_ref, ts_ref, xs_ref, wg_ref, wu_ref, wd_ref, ys_ref):
    del ts_ref
    i = pl.program_id(0)

    @pl.when(tv_ref[i] == 0)
    def _():
        ys_ref[...] = jnp.zeros_like(ys_ref)

    @pl.when(tv_ref[i] != 0)
    def _():
        x = xs_ref[...].astype(BF16)
        acc = jnp.zeros(ys_ref.shape, F32)
        for c in range(wg_ref.shape[1] // MOE_FC):
            cs = slice(c * MOE_FC, (c + 1) * MOE_FC)
            gate = jnp.dot(x, wg_ref[:, cs], preferred_element_type=F32)
            up = jnp.dot(x, wu_ref[:, cs], preferred_element_type=F32)
            act = (gate * _sigmoid(gate) * up).astype(BF16)
            acc = acc + jnp.dot(act, wd_ref[cs, :], preferred_element_type=F32)
        ys_ref[...] = acc


def _moe(tile_expert, tile_valid, tile_src, xs, wg, wu, wd, rows):
    d_exp = wg.shape[2]
    wspec = lambda shape: pl.BlockSpec(shape, lambda i, te, tv, ts: (te[i], 0, 0), pipeline_mode=pl.Buffered(1))
    grid_spec = pltpu.PrefetchScalarGridSpec(
        num_scalar_prefetch=3,
        grid=(rows // MOE_TM,),
        in_specs=[
            pl.BlockSpec((MOE_TM, D_MODEL), lambda i, te, tv, ts: (ts[i], 0)),
            wspec((None, D_MODEL, d_exp)), wspec((None, D_MODEL, d_exp)), wspec((None, d_exp, D_MODEL)),
        ],
        out_specs=pl.BlockSpec((MOE_TM, D_MODEL), lambda i, te, tv, ts: (i, 0)),
    )
    return pl.pallas_call(
        _moe_kernel,
        grid_spec=grid_spec,
        out_shape=jax.ShapeDtypeStruct((rows, D_MODEL), F32),
        compiler_params=_cparams(("arbitrary",)),
        name="experts",
    )(tile_expert, tile_valid, tile_src, xs, wg, wu, wd)


def _combine_kernel(pos_ref, next_ref, x_ref, wc_ref, g_ref, ys_ref, out_ref, buf, sems):
    i = pl.program_id(0)
    n = pl.num_programs(0)
    tm = x_ref.shape[0]

    def copy(slot, idx_ref, r, k):
        return pltpu.make_async_copy(ys_ref.at[pl.ds(idx_ref[k, r], 1)], buf.at[slot, k, pl.ds(r, 1)],
                                     sems.at[slot])

    def start_tile(slot, idx_ref):
        def start(r, _):
            copy(slot, idx_ref, r, 0).start()
            copy(slot, idx_ref, r, 1).start()
            return 0
        lax.fori_loop(0, tm, start, 0, unroll=8)

    slot = i % 2

    @pl.when(i == 0)
    def _():
        start_tile(0, pos_ref)

    @pl.when(i + 1 < n)
    def _():
        start_tile(1 - slot, next_ref)

    def wait(r, _):
        copy(slot, pos_ref, r, 0).wait()
        copy(slot, pos_ref, r, 1).wait()
        return 0

    lax.fori_loop(0, tm, wait, 0, unroll=8)
    wc = wc_ref[...]
    y = x_ref[...] + (wc[:, 0:1] * buf[slot, 0] + wc[:, 1:2] * buf[slot, 1])
    out_ref[...] = _rms(y, g_ref[...])


def _combine(pos3, x2, wcols, g, ys):
    tokens = x2.shape[0]
    ntiles = tokens // GATHER_TM
    return pl.pallas_call(
        _combine_kernel,
        grid=(tokens // GATHER_TM,),
        in_specs=[
            pl.BlockSpec((None, TOP_K, GATHER_TM), lambda i: (i, 0, 0), memory_space=pltpu.SMEM),
            pl.BlockSpec((None, TOP_K, GATHER_TM), lambda i: (jnp.minimum(i + 1, ntiles - 1), 0, 0),
                         memory_space=pltpu.SMEM),
            pl.BlockSpec((GATHER_TM, D_MODEL), lambda i: (i, 0)),
            pl.BlockSpec((GATHER_TM, LANES), lambda i: (i, 0)),
            _resident((1, D_MODEL), lambda i: (0, 0)),
            pl.BlockSpec(memory_space=pl.ANY),
        ],
        out_specs=pl.BlockSpec((GATHER_TM, D_MODEL), lambda i: (i, 0)),
        out_shape=jax.ShapeDtypeStruct((tokens, D_MODEL), F32),
        scratch_shapes=[pltpu.VMEM((2, TOP_K, GATHER_TM, D_MODEL), F32), pltpu.SemaphoreType.DMA((2,))],
        compiler_params=_cparams(("arbitrary",)),
        name="combine",
    )(pos3, pos3, x2, wcols, g, ys)


def _rotary_lane_tables(seq):
    half = ROT_DIM // 2
    pos = jnp.arange(seq, dtype=F32)
    inv_freq = ROPE_THETA ** (-jnp.arange(0, ROT_DIM, 2, dtype=F32) / ROT_DIM)
    ang = pos[:, None] * inv_freq[None, :]
    cos, sin = jnp.cos(ang), jnp.sin(ang)
    j = np.arange(LANES) % HEAD_DIM
    idx = j % half
    cosf = jnp.where((j < ROT_DIM)[None, :], cos[:, idx], 1.0)
    sina = jnp.where(((j >= half) & (j < ROT_DIM))[None, :], sin[:, idx], 0.0)
    sinb = jnp.where((j < half)[None, :], -sin[:, idx], 0.0)
    return cosf, sina, sinb


def _mixer(x2, layer, p, tables, batch, seq):
    cuts = np.cumsum([D_RNN, D_RNN, ATT_WIDTH, ATT_WIDTH, ATT_WIDTH])
    w = p["w_in"][layer]
    w_perm = jnp.concatenate([w[:, :cuts[1]], w[:, cuts[4]:], w[:, cuts[1]:cuts[4]]], axis=1).astype(BF16)
    xr, yg, qkv4 = _in_proj(x2, p["norm_mix_g"][layer][None], w_perm, *tables, batch, seq)

    wgate = (0.5 * jnp.concatenate([p["w_rgate"][layer], p["w_igate"][layer]], axis=-1)).astype(BF16)
    hf, hb = _lru(xr.reshape(batch, seq, D_RNN), p["conv_w"][layer], p["conv_b"][layer][None], wgate,
                  0.5 * p["b_rgate"][layer], 0.5 * p["b_igate"][layer], p["lru_lambda"][layer])
    att = _attention(qkv4, batch, seq)

    tokens = batch * seq
    return _merge(x2, hf.reshape(tokens, D_RNN), hb.reshape(tokens, D_RNN), yg, att,
                  p["w_proj_lru"][layer].astype(BF16), p["w_proj_att"][layer].astype(BF16),
                  p["w_out"][layer].astype(BF16))


def _moe_layer(x2, j, p, final_g):
    tokens = x2.shape[0]
    wr = jnp.pad(p["w_router"][j], ((0, 0), (0, LANES - N_EXPERTS)))
    whi = wr.astype(BF16)
    wlo = (wr - whi.astype(F32)).astype(BF16)
    hp, wcols, meta, counts = _router(x2, p["norm_ffn_g"][j * 2 + 1][None], whi, wlo)

    counts = counts[0, :N_EXPERTS]
    padded = ((counts + MOE_TM - 1) // MOE_TM) * MOE_TM
    ends = jnp.cumsum(padded)
    starts = ends - padded
    experts_of = meta[0:TOP_K]
    base = jnp.zeros_like(experts_of)
    for e in range(N_EXPERTS):
        base = base + jnp.where(experts_of == e, starts[e], 0)
    pos = base + meta[TOP_K:2 * TOP_K]
    pos3 = pos.reshape(TOP_K, tokens // GATHER_TM, GATHER_TM).transpose(1, 0, 2)

    rows = TOP_K * tokens + N_EXPERTS * MOE_TM
    tile_start = jnp.arange(rows // MOE_TM, dtype=jnp.int32) * MOE_TM
    tile_expert = jnp.minimum(jnp.sum(tile_start[:, None] >= ends[None, :], axis=1), N_EXPERTS - 1).astype(jnp.int32)
    tile_valid = (tile_start < ends[-1]).astype(jnp.int32)
    tile_src = jnp.minimum(tile_start // MOE_TM, ends[-1] // MOE_TM - 1).astype(jnp.int32)
    pad_start8 = ((starts + counts) // SUBLANES * SUBLANES).astype(jnp.int32)
    pad_start8 = jnp.concatenate([pad_start8, ends[-1:].astype(jnp.int32)])

    xs = _dispatch(pad_start8, pos3, hp, rows)
    ys = _moe(tile_expert, tile_valid, tile_src, xs, p["w_exp_gate"][j].astype(BF16),
              p["w_exp_up"][j].astype(BF16), p["w_exp_down"][j].astype(BF16), rows)
    return _combine(pos3, x2, wcols, final_g[None], ys)


def kernel(x, norm_mix_g, w_in, conv_w, conv_b, w_rgate, b_rgate, w_igate, b_igate, lru_lambda, w_proj_lru,
           w_proj_att, w_out, norm_ffn_g, w_dense_gate, w_dense_up, w_dense_down, w_router, w_exp_gate, w_exp_up,
           w_exp_down, final_norm_g):
    batch, seq, _ = x.shape
    depth = w_in.shape[0]
    assert depth % 2 == 0, "the final RMSNorm is fused into the last (routed) layer"
    p = dict(norm_mix_g=norm_mix_g, w_in=w_in, conv_w=conv_w, conv_b=conv_b, w_rgate=w_rgate, b_rgate=b_rgate,
             w_igate=w_igate, b_igate=b_igate, lru_lambda=lru_lambda, w_proj_lru=w_proj_lru,
             w_proj_att=w_proj_att, w_out=w_out, norm_ffn_g=norm_ffn_g, w_router=w_router,
             w_exp_gate=w_exp_gate, w_exp_up=w_exp_up, w_exp_down=w_exp_down)
    tables = _rotary_lane_tables(seq)
    x2 = x.reshape(batch * seq, D_MODEL)
    for layer in range(depth):
        x2 = _mixer(x2, layer, p, tables, batch, seq)
        j = layer // 2
        if layer % 2 == 0:
            x2 = _ffn(x2, norm_ffn_g[layer][None], w_dense_gate[j].astype(BF16), w_dense_up[j].astype(BF16),
                      w_dense_down[j].astype(BF16))
        else:
            assert layer == depth - 1
            x2 = _moe_layer(x2, j, p, final_norm_g)
    return x2.reshape(batch, seq, D_MODEL)
```

```python
import functools

import jax
import jax.numpy as jnp
import numpy as np
from jax import lax
from jax.experimental import pallas as pl
from jax.experimental.pallas import tpu as pltpu

D_MODEL = 1024
D_RNN = D_MODEL
N_LRU_BLOCKS = 8
LRU_BLOCK = D_RNN // N_LRU_BLOCKS
CONV_WIDTH = 4
LRU_C = 8.0
HEAD_DIM = 64
HEADS_PER_GROUP = 4
WINDOWS = (128, 512, 2048)
DILATIONS = (1, 4, 16)
N_GROUPS = len(WINDOWS)
ATT_WIDTH = N_GROUPS * HEADS_PER_GROUP * HEAD_DIM
ATT_OUT = HEADS_PER_GROUP * HEAD_DIM
ROT_DIM = HEAD_DIM // 4
ROPE_THETA = 500000.0
HALF_KEYS = 64
IN_WIDTH = 2 * D_RNN + 3 * ATT_WIDTH + 2 * D_MODEL
QKV_WIDTH = 3 * ATT_WIDTH
YG_WIDTH = 3 * D_MODEL
N_EXPERTS = 8
TOP_K = 2
RMS_EPS = 1e-6

LANES = 128
SUBLANES = 8
VMEM_LIMIT = 52 * 1024 * 1024

TM = 512
LRU_TS = 256
LRU_CB = 256
ATT_CQ = 128
ATT_KW = ATT_CQ + 2 * HALF_KEYS
MOE_TM = 512
MOE_FC = 512
GATHER_TM = 256

BF16 = jnp.bfloat16
F32 = jnp.float32
LOG2E = float(np.log2(np.e))
LN2 = float(np.log(2.0))
Q_SCALE = HEAD_DIM ** -0.5 * LOG2E
SQRT_FLOOR = 1e-30


def _cparams(sem):
    return pltpu.CompilerParams(dimension_semantics=sem, vmem_limit_bytes=VMEM_LIMIT)


def _resident(shape, index_map):
    return pl.BlockSpec(shape, index_map, pipeline_mode=pl.Buffered(1))


def _sigmoid(x):
    return 0.5 * jnp.tanh(0.5 * x) + 0.5


def _rms(x, g):
    ms = jnp.mean(x * x, axis=-1, keepdims=True)
    return x * lax.rsqrt(ms + RMS_EPS) * g


def _in_proj_kernel(seq_tiles, x_ref, xp_ref, xn_ref, g_ref, w_ref, cw_ref, cb_ref, cos_ref, sa_ref, sb_ref,
                    xc_ref, yg_ref, qkv_ref):
    i = pl.program_id(0)
    tm = x_ref.shape[0]
    halo = xp_ref.shape[0]
    xp = jnp.where((i % seq_tiles) == 0, 0.0, xp_ref[...])
    xn = jnp.where((i % seq_tiles) == seq_tiles - 1, 0.0, xn_ref[...])
    hn_ext = _rms(jnp.concatenate([xp, x_ref[...], xn], axis=0), g_ref[...]).astype(BF16)
    hn = hn_ext[halo:halo + tm]

    def proj(c0, width):
        return jnp.dot(hn, w_ref[:, c0:c0 + width], preferred_element_type=F32)

    cw = cw_ref[...]
    for c in range(D_RNN // 512):
        cs = slice(c * 512, (c + 1) * 512)
        ext = jnp.dot(hn_ext, w_ref[:, cs], preferred_element_type=F32)
        xc = cb_ref[:, cs]
        for j in range(CONV_WIDTH):
            xc = xc + ext[halo - 1 + j:halo - 1 + j + tm] * cw[j:j + 1, cs]
        xc_ref[:, cs] = xc
    for c in range(YG_WIDTH // 512):
        yg_ref[:, c * 512:(c + 1) * 512] = proj(D_RNN + c * 512, 512).astype(BF16)

    cosf, sina, sinb = cos_ref[...], sa_ref[...], sb_ref[...]
    base = D_RNN + YG_WIDTH
    for c in range(QKV_WIDTH // 256):
        acc = proj(base + c * 256, 256)
        if c < 2 * ATT_WIDTH // 256:
            for h in range(2):
                t = acc[:, h * LANES:(h + 1) * LANES]
                r = t * cosf + pltpu.roll(t, 8, 1) * sina + pltpu.roll(t, LANES - 8, 1) * sinb
                qkv_ref[2 * c + h] = r * Q_SCALE if c < ATT_WIDTH // 256 else r
        else:
            for h in range(2):
                qkv_ref[2 * c + h] = acc[:, h * LANES:(h + 1) * LANES]


def _in_proj(x2, g, w_perm, conv_w, conv_b, cosf, sina, sinb, batch, seq):
    tokens = batch * seq
    ns = seq // TM
    halo_blocks = TM // SUBLANES
    return pl.pallas_call(
        functools.partial(_in_proj_kernel, ns),
        grid=(tokens // TM,),
        in_specs=[
            pl.BlockSpec((TM, D_MODEL), lambda i: (i, 0)),
            pl.BlockSpec((SUBLANES, D_MODEL), lambda i: (jnp.maximum(i * halo_blocks - 1, 0), 0)),
            pl.BlockSpec((SUBLANES, D_MODEL),
                         lambda i: (jnp.minimum((i + 1) * halo_blocks, tokens // SUBLANES - 1), 0)),
            _resident((1, D_MODEL), lambda i: (0, 0)),
            _resident((D_MODEL, IN_WIDTH), lambda i: (0, 0)),
            _resident((CONV_WIDTH, D_RNN), lambda i: (0, 0)),
            _resident((1, D_RNN), lambda i: (0, 0)),
            pl.BlockSpec((TM, LANES), lambda i: (i % ns, 0)),
            pl.BlockSpec((TM, LANES), lambda i: (i % ns, 0)),
            pl.BlockSpec((TM, LANES), lambda i: (i % ns, 0)),
        ],
        out_specs=[
            pl.BlockSpec((TM, D_RNN), lambda i: (i, 0)),
            pl.BlockSpec((TM, YG_WIDTH), lambda i: (i, 0)),
            pl.BlockSpec((None, QKV_WIDTH // LANES, TM, LANES), lambda i: (i // ns, 0, i % ns, 0)),
        ],
        out_shape=[
            jax.ShapeDtypeStruct((tokens, D_RNN), F32),
            jax.ShapeDtypeStruct((tokens, YG_WIDTH), BF16),
            jax.ShapeDtypeStruct((batch, QKV_WIDTH // LANES, seq, LANES), F32),
        ],
        compiler_params=_cparams(("parallel",)),
        name="in_proj",
    )(x2, x2, x2, g, w_perm, conv_w, conv_b, cosf, sina, sinb)


def _lru_kernel(xf_ref, xb_ref, wg_ref, br_ref, bi_ref, lam_ref, hf_ref, hb_ref, a_scr, u_scr, h_scr, carry_scr):
    c = pl.program_id(1)
    nb = xf_ref.shape[0]
    ts = xf_ref.shape[1]

    @pl.when(c == 0)
    def _():
        carry_scr[...] = jnp.zeros_like(carry_scr)

    def gates(direction, x_ref):
        xc = jnp.swapaxes(x_ref[...], 0, 1)
        for j in range(LRU_CB // LRU_BLOCK):
            sl = slice(j * LRU_BLOCK, (j + 1) * LRU_BLOCK)
            xcb = xc[:, :, sl].reshape(ts * nb, LRU_BLOCK)
            rg = jnp.dot(xcb.astype(BF16), wg_ref[direction, j], preferred_element_type=F32)
            tr = jnp.tanh(rg[:, :LRU_BLOCK] + br_ref[direction:direction + 1, sl])
            ti = jnp.tanh(rg[:, LRU_BLOCK:] + bi_ref[direction:direction + 1, sl])
            lam = lam_ref[direction:direction + 1, sl]
            c2 = (-0.5 * LRU_C * LOG2E) * (jnp.maximum(-lam, 0.0) + jnp.log(1.0 + jnp.exp(-jnp.abs(lam))))
            a = jnp.exp2(tr * c2 + c2)
            xh = 0.5 * xcb
            gated = ti * xh + xh
            om = 1.0 - a * a
            u = (om * lax.rsqrt(jnp.maximum(om, SQRT_FLOOR))) * gated
            a_scr[direction, :, :, sl] = a.reshape(ts, nb, LRU_BLOCK)
            u_scr[direction, :, :, sl] = u.reshape(ts, nb, LRU_BLOCK)

    gates(0, xf_ref)
    gates(1, xb_ref)

    def step(t, carry):
        hf, hb = carry
        tb = ts - 1 - t
        hf = a_scr[0, t] * hf + u_scr[0, t]
        hb = a_scr[1, tb] * hb + u_scr[1, tb]
        h_scr[0, t] = hf
        h_scr[1, tb] = hb
        return hf, hb

    hf, hb = lax.fori_loop(0, ts, step, (carry_scr[0], carry_scr[1]), unroll=8)
    carry_scr[0] = hf
    carry_scr[1] = hb
    hf_ref[...] = jnp.swapaxes(h_scr[0], 0, 1).astype(hf_ref.dtype)
    hb_ref[...] = jnp.swapaxes(h_scr[1], 0, 1).astype(hb_ref.dtype)


def _lru(xc3, wgate, b_r, b_i, lam):
    batch, seq, _ = xc3.shape
    nc = seq // LRU_TS
    ncb = D_RNN // LRU_CB

    def main(rev):
        return pl.BlockSpec((batch, LRU_TS, LRU_CB), lambda n, c: (0, (nc - 1 - c) if rev else c, n))

    vec2 = pl.BlockSpec((2, LRU_CB), lambda n, c: (0, n))
    out_sds = jax.ShapeDtypeStruct((batch, seq, D_RNN), BF16)
    return pl.pallas_call(
        _lru_kernel,
        grid=(ncb, nc),
        in_specs=[
            main(False), main(True),
            pl.BlockSpec((2, LRU_CB // LRU_BLOCK, LRU_BLOCK, 2 * LRU_BLOCK), lambda n, c: (0, n, 0, 0)),
            vec2, vec2, vec2,
        ],
        out_specs=[main(False), main(True)],
        out_shape=[out_sds, out_sds],
        scratch_shapes=[
            pltpu.VMEM((2, LRU_TS, batch, LRU_CB), F32),
            pltpu.VMEM((2, LRU_TS, batch, LRU_CB), F32),
            pltpu.VMEM((2, LRU_TS, batch, LRU_CB), F32),
            pltpu.VMEM((2, batch, LRU_CB), F32),
        ],
        compiler_params=_cparams(("parallel", "arbitrary")),
        name="lru",
    )(xc3, xc3, wgate, b_r, b_i, lam)


def _attn_kernel(q_ref, k_ref, v_ref, bias_ref, out_ref, acc_o, acc_l):
    g = pl.program_id(1)
    seq = q_ref.shape[1]
    nblk = ATT_OUT // LANES
    lane_head = lax.broadcasted_iota(jnp.int32, (1, ATT_OUT), 1) // HEAD_DIM

    def rows(start, n, d):
        return pl.ds(start, n) if d == 1 else pl.ds(start, n, stride=d)

    def load(ref, start, n, d):
        return jnp.concatenate([ref[j, rows(start, n, d), :] for j in range(nblk)], axis=1)

    def store(ref, start, n, d, val):
        for j in range(nblk):
            ref[j, rows(start, n, d), :] = val[:, j * LANES:(j + 1) * LANES]

    def run_group(gi):
        d = DILATIONS[gi]
        length = seq // d
        chunks_per_residue = length // ATT_CQ

        def chunk(n, _):
            r = n // chunks_per_residue
            qs = (n - r * chunks_per_residue) * ATT_CQ
            ks = jnp.clip(qs - HALF_KEYS, 0, length - ATT_KW)
            q0, k0 = r + d * qs, r + d * ks
            if d == 1:
                q0, k0 = pl.multiple_of(q0, ATT_CQ), pl.multiple_of(k0, HALF_KEYS)
            q = load(q_ref, q0, ATT_CQ, d).astype(BF16)
            kk = load(k_ref, k0, ATT_KW, d).astype(BF16)
            vv = load(v_ref, k0, ATT_KW, d).astype(BF16)
            q4 = jnp.concatenate([jnp.where(lane_head == h, q, jnp.zeros_like(q))
                                  for h in range(HEADS_PER_GROUP)], axis=0)
            s = lax.dot_general(q4, kk, (((1,), (1,)), ((), ())), preferred_element_type=F32)
            s = s + bias_ref[(qs - ks) // HALF_KEYS]
            m = jnp.max(s, axis=1, keepdims=True)
            p = jnp.exp2(s - m)
            l = jnp.sum(p, axis=1, keepdims=True)
            pv = jnp.dot(p.astype(BF16), vv, preferred_element_type=F32)
            inv = 1.0 / l
            lse = m * LN2 + jnp.log(l)
            o = jnp.zeros((ATT_CQ, ATT_OUT), F32)
            lf = jnp.zeros((ATT_CQ, ATT_OUT), F32)
            for h in range(HEADS_PER_GROUP):
                sl = slice(h * ATT_CQ, (h + 1) * ATT_CQ)
                sel = lane_head == h
                o = jnp.where(sel, pv[sl] * inv[sl], o)
                lf = jnp.where(sel, lse[sl], lf)
            if gi == 0:
                store(acc_o, q0, ATT_CQ, d, o)
                store(acc_l, q0, ATT_CQ, d, lf)
            else:
                ao = load(acc_o, q0, ATT_CQ, d)
                al = load(acc_l, q0, ATT_CQ, d)
                top = jnp.maximum(al, lf)
                wa, wg = jnp.exp(al - top), jnp.exp(lf - top)
                den = wa + wg
                store(acc_o, q0, ATT_CQ, d, (wa * ao + wg * o) * (1.0 / den))
                if gi < N_GROUPS - 1:
                    store(acc_l, q0, ATT_CQ, d, top + jnp.log(den))
            return 0

        lax.fori_loop(0, seq // ATT_CQ, chunk, 0, unroll=4)

    for gi in range(N_GROUPS):
        pl.when(g == gi)(functools.partial(run_group, gi))

    @pl.when(g == N_GROUPS - 1)
    def _():
        out_ref[...] = jnp.concatenate([acc_o[j] for j in range(nblk)], axis=1).astype(out_ref.dtype)


def _attention(qkv4, batch, seq):
    per_kind = ATT_WIDTH // ATT_OUT
    nblk = ATT_OUT // LANES

    def spec(kind):
        return pl.BlockSpec((None, nblk, seq, LANES), lambda b, g: (b, kind * per_kind + g, 0, 0))

    rows4 = HEADS_PER_GROUP * ATT_CQ
    qrow = (np.arange(rows4) % ATT_CQ)[None, :, None]
    kcol = np.arange(ATT_KW)[None, None, :]
    shift = (np.arange(3) * HALF_KEYS)[:, None, None]
    bias = jnp.asarray(np.where(np.abs(kcol - shift - qrow) <= HALF_KEYS, 0.0, -np.inf), F32)

    return pl.pallas_call(
        _attn_kernel,
        grid=(batch, N_GROUPS),
        in_specs=[spec(0), spec(1), spec(2), _resident((3, rows4, ATT_KW), lambda b, g: (0, 0, 0))],
        out_specs=pl.BlockSpec((seq, ATT_OUT), lambda b, g: (b, 0)),
        out_shape=jax.ShapeDtypeStruct((batch * seq, ATT_OUT), BF16),
        scratch_shapes=[pltpu.VMEM((nblk, seq, LANES), F32), pltpu.VMEM((nblk, seq, LANES), F32)],
        compiler_params=_cparams(("parallel", "arbitrary")),
        name="attention",
    )(qkv4, qkv4, qkv4, bias)


def _merge_kernel(x_ref, hf_ref, hb_ref, yg_ref, att_ref, wpl_ref, wpa_ref, wo_ref, out_ref):
    y = yg_ref[:, :D_MODEL].astype(F32)
    gelu = 0.5 * y * (1.0 + jnp.tanh(np.sqrt(2.0 / np.pi) * (y + 0.044715 * (y * y * y))))
    lru = ((hf_ref[...].astype(F32) + hb_ref[...].astype(F32)) * gelu).astype(BF16)
    branch_lru = jnp.dot(lru, wpl_ref[...], preferred_element_type=F32)
    branch_att = jnp.dot(att_ref[...], wpa_ref[...], preferred_element_type=F32)

    g_lru = _sigmoid(yg_ref[:, D_MODEL:2 * D_MODEL].astype(F32))
    g_att = _sigmoid(yg_ref[:, 2 * D_MODEL:].astype(F32))
    merged = (g_lru * branch_lru + g_att * branch_att).astype(BF16)
    out_ref[...] = x_ref[...] + jnp.dot(merged, wo_ref[...], preferred_element_type=F32)


def _merge(x2, hf2, hb2, yg, att, wpl, wpa, wo):
    tokens = x2.shape[0]
    tok = lambda w: pl.BlockSpec((TM, w), lambda i: (i, 0))
    return pl.pallas_call(
        _merge_kernel,
        grid=(tokens // TM,),
        in_specs=[tok(D_MODEL), tok(D_RNN), tok(D_RNN), tok(YG_WIDTH), tok(ATT_OUT)] + [
            _resident((D_RNN, D_MODEL), lambda i: (0, 0)),
            _resident((ATT_OUT, D_MODEL), lambda i: (0, 0)),
            _resident((D_MODEL, D_MODEL), lambda i: (0, 0)),
        ],
        out_specs=tok(D_MODEL),
        out_shape=jax.ShapeDtypeStruct((tokens, D_MODEL), F32),
        compiler_params=_cparams(("parallel",)),
        name="merge",
    )(x2, hf2, hb2, yg, att, wpl, wpa, wo)


def _ffn_chunks(d_ff):
    tiles = d_ff // 256
    sizes = [tiles // 3 + (1 if k < tiles % 3 else 0) for k in range(3)]
    out, start = [], 0
    for s in sizes:
        if s:
            out.append((start * 256, s * 256))
            start += s
    return out


def _ffn_kernel(x_ref, g_ref, wg_ref, wu_ref, wd_ref, out_ref):
    x = x_ref[...]
    hn = _rms(x, g_ref[...]).astype(BF16)
    acc = x
    for c0, width in _ffn_chunks(wg_ref.shape[1]):
        gate = jnp.dot(hn, wg_ref[:, c0:c0 + width], preferred_element_type=F32)
        up = jnp.dot(hn, wu_ref[:, c0:c0 + width], preferred_element_type=F32)
        act = (gate * _sigmoid(gate) * up).astype(BF16)
        acc = acc + jnp.dot(act, wd_ref[c0:c0 + width, :], preferred_element_type=F32)
    out_ref[...] = acc


def _ffn(x2, g, wg, wu, wd):
    tokens = x2.shape[0]
    d_ff = wg.shape[1]
    return pl.pallas_call(
        _ffn_kernel,
        grid=(tokens // TM,),
        in_specs=[
            pl.BlockSpec((TM, D_MODEL), lambda i: (i, 0)),
            _resident((1, D_MODEL), lambda i: (0, 0)),
            _resident((D_MODEL, d_ff), lambda i: (0, 0)),
            _resident((D_MODEL, d_ff), lambda i: (0, 0)),
            _resident((d_ff, D_MODEL), lambda i: (0, 0)),
        ],
        out_specs=pl.BlockSpec((TM, D_MODEL), lambda i: (i, 0)),
        out_shape=jax.ShapeDtypeStruct((tokens, D_MODEL), F32),
        compiler_params=_cparams(("parallel",)),
        name="ffn_dense",
    )(x2, g, wg, wu, wd)


ROW_TILE = D_MODEL // LANES
assert ROW_TILE == SUBLANES


def _store_row_tiles(ref, index, rows, val):
    for s in range(ROW_TILE):
        ref[(*index, pl.ds(s, rows, stride=ROW_TILE), slice(None))] = val[:, s * LANES:(s + 1) * LANES]


def _row_tiles(row, n):
    start = row * ROW_TILE
    return pl.ds(start if isinstance(row, int) else pl.multiple_of(start, ROW_TILE), n * ROW_TILE)


def _load_row_tiles(ref, index, rows):
    return jnp.concatenate([ref[(*index, pl.ds(s, rows, stride=ROW_TILE), slice(None))]
                            for s in range(ROW_TILE)], axis=1)


def _router_kernel(x_ref, g_ref, whi_ref, wlo_ref, tri_ref, hp_ref, wc_ref, meta_ref, cnt_ref, carry_scr):
    i = pl.program_id(0)

    @pl.when(i == 0)
    def _():
        carry_scr[...] = jnp.zeros_like(carry_scr)

    hn = _rms(x_ref[...], g_ref[...])
    hi = hn.astype(BF16)
    lo = (hn - hi.astype(F32)).astype(BF16)
    _store_row_tiles(hp_ref, (), hn.shape[0], hn)

    whi = whi_ref[...]
    logits = (jnp.dot(hi, whi, preferred_element_type=F32) + jnp.dot(lo, whi, preferred_element_type=F32)
              + jnp.dot(hi, wlo_ref[...], preferred_element_type=F32))
    tm = logits.shape[0]
    lane = lax.broadcasted_iota(jnp.int32, (tm, LANES), 1).astype(F32)
    lg = jnp.where(lane < N_EXPERTS, logits, -jnp.inf)
    m1 = jnp.max(lg, axis=1, keepdims=True)
    i1 = jnp.min(jnp.where(lg == m1, lane, float(LANES)), axis=1, keepdims=True)
    lg2 = jnp.where(lane == i1, -jnp.inf, lg)
    m2 = jnp.max(lg2, axis=1, keepdims=True)
    i2 = jnp.min(jnp.where(lg2 == m2, lane, float(LANES)), axis=1, keepdims=True)
    e = jnp.exp(m2 - m1)
    w1 = 1.0 / (1.0 + e)
    w2 = e * w1
    wc_ref[...] = jnp.where(lane == 0.0, w1, jnp.where(lane == 1.0, w2, 0.0))

    hot1 = lane == i1
    hot2 = lane == i2
    onehot = jnp.where(hot1 | hot2, 1.0, 0.0).astype(BF16)
    cum = jnp.dot(tri_ref[...], onehot, preferred_element_type=F32)
    tot = carry_scr[...] + cum
    r1 = jnp.sum(jnp.where(hot1, tot - 1.0, 0.0), axis=1, keepdims=True)
    r2 = jnp.sum(jnp.where(hot2, tot - 1.0, 0.0), axis=1, keepdims=True)
    carry_scr[...] = tot[tm - 1:tm, :]
    cnt_ref[...] = tot[tm - 1:tm, :].astype(jnp.int32)

    cols = jnp.where(lane == 0.0, i1, jnp.where(lane == 1.0, i2, jnp.where(lane == 2.0, r1,
                     jnp.where(lane == 3.0, r2, 0.0))))
    meta_ref[...] = cols.T[:SUBLANES, :].astype(jnp.int32)


def _router(x2, g, whi, wlo):
    tokens = x2.shape[0]
    tri = jnp.tril(jnp.ones((TM, TM), F32)).astype(BF16)
    return pl.pallas_call(
        _router_kernel,
        grid=(tokens // TM,),
        in_specs=[
            pl.BlockSpec((TM, D_MODEL), lambda i: (i, 0)),
            _resident((1, D_MODEL), lambda i: (0, 0)),
            _resident((D_MODEL, LANES), lambda i: (0, 0)),
            _resident((D_MODEL, LANES), lambda i: (0, 0)),
            _resident((TM, TM), lambda i: (0, 0)),
        ],
        out_specs=[
            pl.BlockSpec((TM * ROW_TILE, LANES), lambda i: (i, 0)),
            pl.BlockSpec((TM, LANES), lambda i: (i, 0)),
            pl.BlockSpec((SUBLANES, TM), lambda i: (0, i)),
            pl.BlockSpec((1, LANES), lambda i: (0, 0)),
        ],
        out_shape=[
            jax.ShapeDtypeStruct((tokens * ROW_TILE, LANES), F32),
            jax.ShapeDtypeStruct((tokens, LANES), F32),
            jax.ShapeDtypeStruct((SUBLANES, tokens), jnp.int32),
            jax.ShapeDtypeStruct((1, LANES), jnp.int32),
        ],
        scratch_shapes=[pltpu.VMEM((1, LANES), F32)],
        compiler_params=_cparams(("arbitrary",)),
        name="router",
    )(x2, g, whi, wlo, tri)


def _dispatch_kernel(pad_ref, pos_ref, prev_ref, hp_ref, xs_ref, stage, zeros, sems, zsem):
    i = pl.program_id(0)
    n = pl.num_programs(0)
    tm = pos_ref.shape[1]

    @pl.when(i == 0)
    def _():
        zeros[...] = jnp.zeros_like(zeros)
        fill_rows = zeros.shape[0] // ROW_TILE
        used = pad_ref[N_EXPERTS]
        total = xs_ref.shape[0] // ROW_TILE - fill_rows

        def fill(start):
            return pltpu.make_async_copy(zeros, xs_ref.at[_row_tiles(start, fill_rows)], zsem)

        def tail(t):
            return pltpu.make_async_copy(zeros.at[_row_tiles(0, MOE_TM)], xs_ref.at[_row_tiles(t * MOE_TM, MOE_TM)],
                                         zsem)

        def run(cp):
            cp.start()
            cp.wait()

        for e in range(N_EXPERTS):
            run(fill(pad_ref[e]))
        run(fill(total))
        for t in range(total // MOE_TM - N_EXPERTS, total // MOE_TM):
            pl.when(t * MOE_TM >= used)(functools.partial(run, tail(t)))

    def copy(tile, idx_ref, r, k):
        slot = tile % 2
        return pltpu.make_async_copy(stage.at[slot, _row_tiles(r, 1)], xs_ref.at[_row_tiles(idx_ref[k, r], 1)],
                                     sems.at[slot])

    stage[i % 2] = hp_ref[...]

    def start(r, _):
        copy(i, pos_ref, r, 0).start(priority=0)
        copy(i, pos_ref, r, 1).start(priority=1)
        return 0

    def wait_tile(tile, idx_ref):
        def wait(r, _):
            copy(tile, idx_ref, r, 0).wait()
            copy(tile, idx_ref, r, 1).wait()
            return 0
        lax.fori_loop(0, tm, wait, 0, unroll=8)

    lax.fori_loop(0, tm, start, 0, unroll=8)

    @pl.when(i > 0)
    def _():
        wait_tile(i - 1, prev_ref)

    @pl.when(i == n - 1)
    def _():
        wait_tile(i, pos_ref)


FILL_ROWS = MOE_TM + SUBLANES


def _dispatch(pad_start8, pos3, hp, rows):
    tokens = hp.shape[0] // ROW_TILE
    return pl.pallas_call(
        _dispatch_kernel,
        grid=(tokens // GATHER_TM,),
        in_specs=[
            pl.BlockSpec(memory_space=pltpu.SMEM),
            pl.BlockSpec((None, TOP_K, GATHER_TM), lambda i: (i, 0, 0), memory_space=pltpu.SMEM),
            pl.BlockSpec((None, TOP_K, GATHER_TM), lambda i: (jnp.maximum(i - 1, 0), 0, 0),
                         memory_space=pltpu.SMEM),
            pl.BlockSpec((GATHER_TM * ROW_TILE, LANES), lambda i: (i, 0)),
        ],
        out_specs=pl.BlockSpec(memory_space=pl.ANY),
        out_shape=jax.ShapeDtypeStruct(((rows + FILL_ROWS) * ROW_TILE, LANES), F32),
        scratch_shapes=[pltpu.VMEM((2, GATHER_TM * ROW_TILE, LANES), F32),
                        pltpu.VMEM((FILL_ROWS * ROW_TILE, LANES), F32),
                        pltpu.SemaphoreType.DMA((2,)), pltpu.SemaphoreType.DMA],
        compiler_params=_cparams(("arbitrary",)),
        name="dispatch",
    )(pad_start8, pos3, pos3, hp)


def _moe_kernel(te_ref, tv_ref, ts_ref, xs_ref, wg_ref, wu_ref, wd_ref, ys_ref):
    del ts_ref
    i = pl.program_id(0)

    @pl.when(tv_ref[i] == 0)
    def _():
        ys_ref[...] = jnp.zeros_like(ys_ref)

    @pl.when(tv_ref[i] != 0)
    def _():
        x = _load_row_tiles(xs_ref, (), MOE_TM).astype(BF16)
        acc = jnp.zeros((MOE_TM, D_MODEL), F32)
        for c in range(wg_ref.shape[1] // MOE_FC):
            cs = slice(c * MOE_FC, (c + 1) * MOE_FC)
            gate = jnp.dot(x, wg_ref[:, cs], preferred_element_type=F32)
            up = jnp.dot(x, wu_ref[:, cs], preferred_element_type=F32)
            act = (gate * _sigmoid(gate) * up).astype(BF16)
            acc = acc + jnp.dot(act, wd_ref[cs, :], preferred_element_type=F32)
        _store_row_tiles(ys_ref, (), MOE_TM, acc)


def _moe(tile_expert, tile_valid, tile_src, xs, wg, wu, wd, rows):
    d_exp = wg.shape[2]
    wspec = lambda shape: pl.BlockSpec(shape, lambda i, te, tv, ts: (te[i], 0, 0), pipeline_mode=pl.Buffered(1))
    grid_spec = pltpu.PrefetchScalarGridSpec(
        num_scalar_prefetch=3,
        grid=(rows // MOE_TM,),
        in_specs=[
            pl.BlockSpec((MOE_TM * ROW_TILE, LANES), lambda i, te, tv, ts: (ts[i], 0)),
            wspec((None, D_MODEL, d_exp)), wspec((None, D_MODEL, d_exp)), wspec((None, d_exp, D_MODEL)),
        ],
        out_specs=pl.BlockSpec((MOE_TM * ROW_TILE, LANES), lambda i, te, tv, ts: (i, 0)),
    )
    return pl.pallas_call(
        _moe_kernel,
        grid_spec=grid_spec,
        out_shape=jax.ShapeDtypeStruct((rows * ROW_TILE, LANES), F32),
        compiler_params=_cparams(("arbitrary",)),
        name="experts",
    )(tile_expert, tile_valid, tile_src, xs, wg, wu, wd)


def _combine_kernel(pos_ref, next_ref, x_ref, wc_ref, g_ref, ys_ref, out_ref, buf, sems):
    i = pl.program_id(0)
    n = pl.num_programs(0)
    tm = x_ref.shape[0]

    def copy(slot, idx_ref, r, k):
        return pltpu.make_async_copy(ys_ref.at[_row_tiles(idx_ref[k, r], 1)], buf.at[slot, k, _row_tiles(r, 1)],
                                     sems.at[slot])

    def start_tile(slot, idx_ref):
        def start(r, _):
            copy(slot, idx_ref, r, 0).start(priority=0)
            copy(slot, idx_ref, r, 1).start(priority=1)
            return 0
        lax.fori_loop(0, tm, start, 0, unroll=8)

    slot = i % 2

    @pl.when(i == 0)
    def _():
        start_tile(0, pos_ref)

    @pl.when(i + 1 < n)
    def _():
        start_tile(1 - slot, next_ref)

    def wait(r, _):
        copy(slot, pos_ref, r, 0).wait()
        copy(slot, pos_ref, r, 1).wait()
        return 0

    lax.fori_loop(0, tm, wait, 0, unroll=8)
    wc = wc_ref[...]
    y0 = _load_row_tiles(buf, (slot, 0), tm)
    y1 = _load_row_tiles(buf, (slot, 1), tm)
    y = x_ref[...] + (wc[:, 0:1] * y0 + wc[:, 1:2] * y1)
    out_ref[...] = _rms(y, g_ref[...])


def _combine(pos3, x2, wcols, g, ys):
    tokens = x2.shape[0]
    ntiles = tokens // GATHER_TM
    return pl.pallas_call(
        _combine_kernel,
        grid=(tokens // GATHER_TM,),
        in_specs=[
            pl.BlockSpec((None, TOP_K, GATHER_TM), lambda i: (i, 0, 0), memory_space=pltpu.SMEM),
            pl.BlockSpec((None, TOP_K, GATHER_TM), lambda i: (jnp.minimum(i + 1, ntiles - 1), 0, 0),
                         memory_space=pltpu.SMEM),
            pl.BlockSpec((GATHER_TM, D_MODEL), lambda i: (i, 0)),
            pl.BlockSpec((GATHER_TM, LANES), lambda i: (i, 0)),
            _resident((1, D_MODEL), lambda i: (0, 0)),
            pl.BlockSpec(memory_space=pl.ANY),
        ],
        out_specs=pl.BlockSpec((GATHER_TM, D_MODEL), lambda i: (i, 0)),
        out_shape=jax.ShapeDtypeStruct((tokens, D_MODEL), F32),
        scratch_shapes=[pltpu.VMEM((2, TOP_K, GATHER_TM * ROW_TILE, LANES), F32), pltpu.SemaphoreType.DMA((2,))],
        compiler_params=_cparams(("arbitrary",)),
        name="combine",
    )(pos3, pos3, x2, wcols, g, ys)


def _rotary_lane_tables(seq):
    half = ROT_DIM // 2
    pos = jnp.arange(seq, dtype=F32)
    inv_freq = ROPE_THETA ** (-jnp.arange(0, ROT_DIM, 2, dtype=F32) / ROT_DIM)
    ang = pos[:, None] * inv_freq[None, :]
    cos, sin = jnp.cos(ang), jnp.sin(ang)
    j = np.arange(LANES) % HEAD_DIM
    idx = j % half
    cosf = jnp.where((j < ROT_DIM)[None, :], cos[:, idx], 1.0)
    sina = jnp.where(((j >= half) & (j < ROT_DIM))[None, :], sin[:, idx], 0.0)
    sinb = jnp.where((j < half)[None, :], -sin[:, idx], 0.0)
    return cosf, sina, sinb


def _mixer(x2, layer, p, tables, batch, seq):
    cuts = np.cumsum([D_RNN, D_RNN, ATT_WIDTH, ATT_WIDTH, ATT_WIDTH])
    w = p["w_in"][layer]
    w_perm = jnp.concatenate([w[:, :cuts[1]], w[:, cuts[4]:], w[:, cuts[1]:cuts[4]]], axis=1).astype(BF16)
    xc, yg, qkv4 = _in_proj(x2, p["norm_mix_g"][layer][None], w_perm, p["conv_w"][layer],
                            p["conv_b"][layer][None], *tables, batch, seq)

    wgate = (0.5 * jnp.concatenate([p["w_rgate"][layer], p["w_igate"][layer]], axis=-1)).astype(BF16)
    hf, hb = _lru(xc.reshape(batch, seq, D_RNN), wgate, 0.5 * p["b_rgate"][layer], 0.5 * p["b_igate"][layer],
                  p["lru_lambda"][layer])
    att = _attention(qkv4, batch, seq)

    tokens = batch * seq
    return _merge(x2, hf.reshape(tokens, D_RNN), hb.reshape(tokens, D_RNN), yg, att,
                  p["w_proj_lru"][layer].astype(BF16), p["w_proj_att"][layer].astype(BF16),
                  p["w_out"][layer].astype(BF16))


def _moe_layer(x2, j, p, final_g):
    tokens = x2.shape[0]
    wr = jnp.pad(p["w_router"][j], ((0, 0), (0, LANES - N_EXPERTS)))
    whi = wr.astype(BF16)
    wlo = (wr - whi.astype(F32)).astype(BF16)
    hp, wcols, meta, counts = _router(x2, p["norm_ffn_g"][j * 2 + 1][None], whi, wlo)

    counts = counts[0, :N_EXPERTS]
    padded = ((counts + MOE_TM - 1) // MOE_TM) * MOE_TM
    ends = jnp.cumsum(padded)
    starts = ends - padded
    experts_of = meta[0:TOP_K]
    base = jnp.zeros_like(experts_of)
    for e in range(N_EXPERTS):
        base = base + jnp.where(experts_of == e, starts[e], 0)
    pos = base + meta[TOP_K:2 * TOP_K]
    pos3 = pos.reshape(TOP_K, tokens // GATHER_TM, GATHER_TM).transpose(1, 0, 2)

    rows = TOP_K * tokens + N_EXPERTS * MOE_TM
    tile_start = jnp.arange(rows // MOE_TM, dtype=jnp.int32) * MOE_TM
    tile_expert = jnp.minimum(jnp.sum(tile_start[:, None] >= ends[None, :], axis=1), N_EXPERTS - 1).astype(jnp.int32)
    tile_valid = (tile_start < ends[-1]).astype(jnp.int32)
    tile_src = jnp.minimum(tile_start // MOE_TM, ends[-1] // MOE_TM - 1).astype(jnp.int32)
    pad_start8 = ((starts + counts) // SUBLANES * SUBLANES).astype(jnp.int32)
    pad_start8 = jnp.concatenate([pad_start8, ends[-1:].astype(jnp.int32)])

    xs = _dispatch(pad_start8, pos3, hp, rows)
    ys = _moe(tile_expert, tile_valid, tile_src, xs, p["w_exp_gate"][j].astype(BF16),
              p["w_exp_up"][j].astype(BF16), p["w_exp_down"][j].astype(BF16), rows)
    return _combine(pos3, x2, wcols, final_g[None], ys)


def kernel(x, norm_mix_g, w_in, conv_w, conv_b, w_rgate, b_rgate, w_igate, b_igate, lru_lambda, w_proj_lru,
           w_proj_att, w_out, norm_ffn_g, w_dense_gate, w_dense_up, w_dense_down, w_router, w_exp_gate, w_exp_up,
           w_exp_down, final_norm_g):
    batch, seq, _ = x.shape
    depth = w_in.shape[0]
    assert depth % 2 == 0, "the final RMSNorm is fused into the last (routed) layer"
    p = dict(norm_mix_g=norm_mix_g, w_in=w_in, conv_w=conv_w, conv_b=conv_b, w_rgate=w_rgate, b_rgate=b_rgate,
             w_igate=w_igate, b_igate=b_igate, lru_lambda=lru_lambda, w_proj_lru=w_proj_lru,
             w_proj_att=w_proj_att, w_out=w_out, norm_ffn_g=norm_ffn_g, w_router=w_router,
             w_exp_gate=w_exp_gate, w_exp_up=w_exp_up, w_exp_down=w_exp_down)
    tables = _rotary_lane_tables(seq)
    x2 = x.reshape(batch * seq, D_MODEL)
    for layer in range(depth):
        x2 = _mixer(x2, layer, p, tables, batch, seq)
        j = layer // 2
        if layer % 2 == 0:
            x2 = _ffn(x2, norm_ffn_g[layer][None], w_dense_gate[j].astype(BF16), w_dense_up[j].astype(BF16),
                      w_dense_down[j].astype(BF16))
        else:
            assert layer == depth - 1
            x2 = _moe_layer(x2, j, p, final_norm_g)
    return x2.reshape(batch, seq, D_MODEL)
```

```python
import functools

import jax
import jax.numpy as jnp
import numpy as np
from jax import lax
from jax.experimental import pallas as pl
from jax.experimental.pallas import tpu as pltpu

D_MODEL = 1024
D_RNN = D_MODEL
N_LRU_BLOCKS = 8
LRU_BLOCK = D_RNN // N_LRU_BLOCKS
CONV_WIDTH = 4
LRU_C = 8.0
HEAD_DIM = 64
HEADS_PER_GROUP = 4
WINDOWS = (128, 512, 2048)
DILATIONS = (1, 4, 16)
N_GROUPS = len(WINDOWS)
ATT_WIDTH = N_GROUPS * HEADS_PER_GROUP * HEAD_DIM
ATT_OUT = HEADS_PER_GROUP * HEAD_DIM
ROT_DIM = HEAD_DIM // 4
ROPE_THETA = 500000.0
HALF_KEYS = 64
IN_WIDTH = 2 * D_RNN + 3 * ATT_WIDTH + 2 * D_MODEL
QKV_WIDTH = 3 * ATT_WIDTH
YG_WIDTH = 3 * D_MODEL
N_EXPERTS = 8
TOP_K = 2
RMS_EPS = 1e-6

LANES = 128
SUBLANES = 8
VMEM_LIMIT = 52 * 1024 * 1024
FUSED_VMEM_LIMIT = 58 * 1024 * 1024

TM = 512
LRU_TS = 256
LRU_CB = 256
ATT_CQ = 128
ATT_KW = ATT_CQ + 2 * HALF_KEYS
MOE_TM = 512
MOE_FC = 512
GATHER_TM = 256

BF16 = jnp.bfloat16
F32 = jnp.float32
LOG2E = float(np.log2(np.e))
LN2 = float(np.log(2.0))
Q_SCALE = HEAD_DIM ** -0.5 * LOG2E
SQRT_FLOOR = 1e-30


def _cparams(sem):
    return pltpu.CompilerParams(dimension_semantics=sem, vmem_limit_bytes=VMEM_LIMIT)


def _resident(shape, index_map):
    return pl.BlockSpec(shape, index_map, pipeline_mode=pl.Buffered(1))


def _sigmoid(x):
    return 0.5 * jnp.tanh(0.5 * x) + 0.5


def _rms(x, g):
    ms = jnp.mean(x * x, axis=-1, keepdims=True)
    return x * lax.rsqrt(ms + RMS_EPS) * g


def _in_proj_kernel(seq_tiles, x_ref, xp_ref, xn_ref, g_ref, w_ref, cw_ref, cb_ref, cos_ref, sa_ref, sb_ref,
                    xc_ref, yg_ref, qkv_ref):
    i = pl.program_id(0)
    tm = x_ref.shape[0]
    halo = xp_ref.shape[0]
    xp = jnp.where((i % seq_tiles) == 0, 0.0, xp_ref[...])
    xn = jnp.where((i % seq_tiles) == seq_tiles - 1, 0.0, xn_ref[...])
    hn_ext = _rms(jnp.concatenate([xp, x_ref[...], xn], axis=0), g_ref[...]).astype(BF16)
    hn = hn_ext[halo:halo + tm]

    def proj(c0, width):
        return jnp.dot(hn, w_ref[:, c0:c0 + width], preferred_element_type=F32)

    cw = cw_ref[...]
    for c in range(D_RNN // 512):
        cs = slice(c * 512, (c + 1) * 512)
        ext = jnp.dot(hn_ext, w_ref[:, cs], preferred_element_type=F32)
        xc = cb_ref[:, cs]
        for j in range(CONV_WIDTH):
            xc = xc + ext[halo - 1 + j:halo - 1 + j + tm] * cw[j:j + 1, cs]
        xc_ref[:, cs] = xc
    for c in range(YG_WIDTH // 512):
        yg_ref[:, c * 512:(c + 1) * 512] = proj(D_RNN + c * 512, 512).astype(BF16)

    cosf, sina, sinb = cos_ref[...], sa_ref[...], sb_ref[...]
    base = D_RNN + YG_WIDTH
    for c in range(QKV_WIDTH // 256):
        acc = proj(base + c * 256, 256)
        if c < 2 * ATT_WIDTH // 256:
            for h in range(2):
                t = acc[:, h * LANES:(h + 1) * LANES]
                r = t * cosf + pltpu.roll(t, 8, 1) * sina + pltpu.roll(t, LANES - 8, 1) * sinb
                qkv_ref[2 * c + h] = r * Q_SCALE if c < ATT_WIDTH // 256 else r
        else:
            for h in range(2):
                qkv_ref[2 * c + h] = acc[:, h * LANES:(h + 1) * LANES]


def _in_proj(x2, g, w_perm, conv_w, conv_b, cosf, sina, sinb, batch, seq):
    tokens = batch * seq
    ns = seq // TM
    halo_blocks = TM // SUBLANES
    return pl.pallas_call(
        functools.partial(_in_proj_kernel, ns),
        grid=(tokens // TM,),
        in_specs=[
            pl.BlockSpec((TM, D_MODEL), lambda i: (i, 0)),
            pl.BlockSpec((SUBLANES, D_MODEL), lambda i: (jnp.maximum(i * halo_blocks - 1, 0), 0)),
            pl.BlockSpec((SUBLANES, D_MODEL),
                         lambda i: (jnp.minimum((i + 1) * halo_blocks, tokens // SUBLANES - 1), 0)),
            _resident((1, D_MODEL), lambda i: (0, 0)),
            _resident((D_MODEL, IN_WIDTH), lambda i: (0, 0)),
            _resident((CONV_WIDTH, D_RNN), lambda i: (0, 0)),
            _resident((1, D_RNN), lambda i: (0, 0)),
            pl.BlockSpec((TM, LANES), lambda i: (i % ns, 0)),
            pl.BlockSpec((TM, LANES), lambda i: (i % ns, 0)),
            pl.BlockSpec((TM, LANES), lambda i: (i % ns, 0)),
        ],
        out_specs=[
            pl.BlockSpec((TM, D_RNN), lambda i: (i, 0)),
            pl.BlockSpec((TM, YG_WIDTH), lambda i: (i, 0)),
            pl.BlockSpec((None, QKV_WIDTH // LANES, TM, LANES), lambda i: (i // ns, 0, i % ns, 0)),
        ],
        out_shape=[
            jax.ShapeDtypeStruct((tokens, D_RNN), F32),
            jax.ShapeDtypeStruct((tokens, YG_WIDTH), BF16),
            jax.ShapeDtypeStruct((batch, QKV_WIDTH // LANES, seq, LANES), F32),
        ],
        compiler_params=_cparams(("parallel",)),
        name="in_proj",
    )(x2, x2, x2, g, w_perm, conv_w, conv_b, cosf, sina, sinb)


def _lru_kernel(xf_ref, xb_ref, wg_ref, br_ref, bi_ref, lam_ref, hf_ref, hb_ref, a_scr, u_scr, h_scr, carry_scr):
    c = pl.program_id(1)
    nb = xf_ref.shape[0]
    ts = xf_ref.shape[1]

    @pl.when(c == 0)
    def _():
        carry_scr[...] = jnp.zeros_like(carry_scr)

    def gates(direction, x_ref):
        xc = jnp.swapaxes(x_ref[...], 0, 1)
        for j in range(LRU_CB // LRU_BLOCK):
            sl = slice(j * LRU_BLOCK, (j + 1) * LRU_BLOCK)
            xcb = xc[:, :, sl].reshape(ts * nb, LRU_BLOCK)
            rg = jnp.dot(xcb.astype(BF16), wg_ref[direction, j], preferred_element_type=F32)
            tr = jnp.tanh(rg[:, :LRU_BLOCK] + br_ref[direction:direction + 1, sl])
            ti = jnp.tanh(rg[:, LRU_BLOCK:] + bi_ref[direction:direction + 1, sl])
            lam = lam_ref[direction:direction + 1, sl]
            c2 = (-0.5 * LRU_C * LOG2E) * (jnp.maximum(-lam, 0.0) + jnp.log(1.0 + jnp.exp(-jnp.abs(lam))))
            a = jnp.exp2(tr * c2 + c2)
            xh = 0.5 * xcb
            gated = ti * xh + xh
            om = 1.0 - a * a
            u = (om * lax.rsqrt(jnp.maximum(om, SQRT_FLOOR))) * gated
            a_scr[direction, :, :, sl] = a.reshape(ts, nb, LRU_BLOCK)
            u_scr[direction, :, :, sl] = u.reshape(ts, nb, LRU_BLOCK)

    gates(0, xf_ref)
    gates(1, xb_ref)

    def step(t, carry):
        hf, hb = carry
        tb = ts - 1 - t
        hf = a_scr[0, t] * hf + u_scr[0, t]
        hb = a_scr[1, tb] * hb + u_scr[1, tb]
        h_scr[0, t] = hf
        h_scr[1, tb] = hb
        return hf, hb

    hf, hb = lax.fori_loop(0, ts, step, (carry_scr[0], carry_scr[1]), unroll=8)
    carry_scr[0] = hf
    carry_scr[1] = hb
    hf_ref[...] = jnp.swapaxes(h_scr[0], 0, 1).astype(hf_ref.dtype)
    hb_ref[...] = jnp.swapaxes(h_scr[1], 0, 1).astype(hb_ref.dtype)


def _lru(xc3, wgate, b_r, b_i, lam):
    batch, seq, _ = xc3.shape
    nc = seq // LRU_TS
    ncb = D_RNN // LRU_CB

    def main(rev):
        return pl.BlockSpec((batch, LRU_TS, LRU_CB), lambda n, c: (0, (nc - 1 - c) if rev else c, n))

    vec2 = pl.BlockSpec((2, LRU_CB), lambda n, c: (0, n))
    out_sds = jax.ShapeDtypeStruct((batch, seq, D_RNN), BF16)
    return pl.pallas_call(
        _lru_kernel,
        grid=(ncb, nc),
        in_specs=[
            main(False), main(True),
            pl.BlockSpec((2, LRU_CB // LRU_BLOCK, LRU_BLOCK, 2 * LRU_BLOCK), lambda n, c: (0, n, 0, 0)),
            vec2, vec2, vec2,
        ],
        out_specs=[main(False), main(True)],
        out_shape=[out_sds, out_sds],
        scratch_shapes=[
            pltpu.VMEM((2, LRU_TS, batch, LRU_CB), F32),
            pltpu.VMEM((2, LRU_TS, batch, LRU_CB), F32),
            pltpu.VMEM((2, LRU_TS, batch, LRU_CB), F32),
            pltpu.VMEM((2, batch, LRU_CB), F32),
        ],
        compiler_params=_cparams(("parallel", "arbitrary")),
        name="lru",
    )(xc3, xc3, wgate, b_r, b_i, lam)


def _attn_kernel(q_ref, k_ref, v_ref, bias_ref, out_ref, acc_o, acc_l):
    g = pl.program_id(1)
    seq = q_ref.shape[1]
    nblk = ATT_OUT // LANES
    lane_head = lax.broadcasted_iota(jnp.int32, (1, ATT_OUT), 1) // HEAD_DIM

    def rows(start, n, d):
        return pl.ds(start, n) if d == 1 else pl.ds(start, n, stride=d)

    def load(ref, start, n, d):
        return jnp.concatenate([ref[j, rows(start, n, d), :] for j in range(nblk)], axis=1)

    def store(ref, start, n, d, val):
        for j in range(nblk):
            ref[j, rows(start, n, d), :] = val[:, j * LANES:(j + 1) * LANES]

    def run_group(gi):
        d = DILATIONS[gi]
        length = seq // d
        chunks_per_residue = length // ATT_CQ

        def chunk(n, _):
            r = n // chunks_per_residue
            qs = (n - r * chunks_per_residue) * ATT_CQ
            ks = jnp.clip(qs - HALF_KEYS, 0, length - ATT_KW)
            q0, k0 = r + d * qs, r + d * ks
            if d == 1:
                q0, k0 = pl.multiple_of(q0, ATT_CQ), pl.multiple_of(k0, HALF_KEYS)
            q = load(q_ref, q0, ATT_CQ, d).astype(BF16)
            kk = load(k_ref, k0, ATT_KW, d).astype(BF16)
            vv = load(v_ref, k0, ATT_KW, d).astype(BF16)
            q4 = jnp.concatenate([jnp.where(lane_head == h, q, jnp.zeros_like(q))
                                  for h in range(HEADS_PER_GROUP)], axis=0)
            s = lax.dot_general(q4, kk, (((1,), (1,)), ((), ())), preferred_element_type=F32)
            s = s + bias_ref[(qs - ks) // HALF_KEYS]
            m = jnp.max(s, axis=1, keepdims=True)
            p = jnp.exp2(s - m)
            l = jnp.sum(p, axis=1, keepdims=True)
            pv = jnp.dot(p.astype(BF16), vv, preferred_element_type=F32)
            inv = 1.0 / l
            lse = m * LN2 + jnp.log(l)
            o = jnp.zeros((ATT_CQ, ATT_OUT), F32)
            lf = jnp.zeros((ATT_CQ, ATT_OUT), F32)
            for h in range(HEADS_PER_GROUP):
                sl = slice(h * ATT_CQ, (h + 1) * ATT_CQ)
                sel = lane_head == h
                o = jnp.where(sel, pv[sl] * inv[sl], o)
                lf = jnp.where(sel, lse[sl], lf)
            if gi == 0:
                store(acc_o, q0, ATT_CQ, d, o)
                store(acc_l, q0, ATT_CQ, d, lf)
            else:
                ao = load(acc_o, q0, ATT_CQ, d)
                al = load(acc_l, q0, ATT_CQ, d)
                top = jnp.maximum(al, lf)
                wa, wg = jnp.exp(al - top), jnp.exp(lf - top)
                den = wa + wg
                store(acc_o, q0, ATT_CQ, d, (wa * ao + wg * o) * (1.0 / den))
                if gi < N_GROUPS - 1:
                    store(acc_l, q0, ATT_CQ, d, top + jnp.log(den))
            return 0

        lax.fori_loop(0, seq // ATT_CQ, chunk, 0, unroll=8)

    for gi in range(N_GROUPS):
        pl.when(g == gi)(functools.partial(run_group, gi))

    @pl.when(g == N_GROUPS - 1)
    def _():
        out_ref[...] = jnp.concatenate([acc_o[j] for j in range(nblk)], axis=1).astype(out_ref.dtype)


def _attention(qkv4, batch, seq):
    per_kind = ATT_WIDTH // ATT_OUT
    nblk = ATT_OUT // LANES

    def spec(kind):
        return pl.BlockSpec((None, nblk, seq, LANES), lambda b, g: (b, kind * per_kind + g, 0, 0))

    rows4 = HEADS_PER_GROUP * ATT_CQ
    qrow = (np.arange(rows4) % ATT_CQ)[None, :, None]
    kcol = np.arange(ATT_KW)[None, None, :]
    shift = (np.arange(3) * HALF_KEYS)[:, None, None]
    bias = jnp.asarray(np.where(np.abs(kcol - shift - qrow) <= HALF_KEYS, 0.0, -np.inf), F32)

    return pl.pallas_call(
        _attn_kernel,
        grid=(batch, N_GROUPS),
        in_specs=[spec(0), spec(1), spec(2), _resident((3, rows4, ATT_KW), lambda b, g: (0, 0, 0))],
        out_specs=pl.BlockSpec((seq, ATT_OUT), lambda b, g: (b, 0)),
        out_shape=jax.ShapeDtypeStruct((batch * seq, ATT_OUT), BF16),
        scratch_shapes=[pltpu.VMEM((nblk, seq, LANES), F32), pltpu.VMEM((nblk, seq, LANES), F32)],
        compiler_params=_cparams(("parallel", "arbitrary")),
        name="attention",
    )(qkv4, qkv4, qkv4, bias)


def _merge_body(x_ref, hf_ref, hb_ref, yg_ref, att_ref, wpl_ref, wpa_ref, wo_ref):
    y = yg_ref[:, :D_MODEL].astype(F32)
    gelu = 0.5 * y * (1.0 + jnp.tanh(np.sqrt(2.0 / np.pi) * (y + 0.044715 * (y * y * y))))
    lru = ((hf_ref[...].astype(F32) + hb_ref[...].astype(F32)) * gelu).astype(BF16)
    branch_lru = jnp.dot(lru, wpl_ref[...], preferred_element_type=F32)
    branch_att = jnp.dot(att_ref[...], wpa_ref[...], preferred_element_type=F32)

    g_lru = _sigmoid(yg_ref[:, D_MODEL:2 * D_MODEL].astype(F32))
    g_att = _sigmoid(yg_ref[:, 2 * D_MODEL:].astype(F32))
    merged = (g_lru * branch_lru + g_att * branch_att).astype(BF16)
    return x_ref[...] + jnp.dot(merged, wo_ref[...], preferred_element_type=F32)


def _merge_specs():
    tok = lambda w: pl.BlockSpec((TM, w), lambda i: (i, 0))
    return [tok(D_MODEL), tok(D_RNN), tok(D_RNN), tok(YG_WIDTH), tok(ATT_OUT),
            _resident((D_RNN, D_MODEL), lambda i: (0, 0)),
            _resident((ATT_OUT, D_MODEL), lambda i: (0, 0)),
            _resident((D_MODEL, D_MODEL), lambda i: (0, 0))]


def _merge_kernel(*refs):
    refs[8][...] = _merge_body(*refs[:8])


def _merge(merge_args):
    tokens = merge_args[0].shape[0]
    return pl.pallas_call(
        _merge_kernel,
        grid=(tokens // TM,),
        in_specs=_merge_specs(),
        out_specs=pl.BlockSpec((TM, D_MODEL), lambda i: (i, 0)),
        out_shape=jax.ShapeDtypeStruct((tokens, D_MODEL), F32),
        compiler_params=_cparams(("parallel",)),
        name="merge",
    )(*merge_args)


def _ffn_chunks(d_ff):
    tiles = d_ff // 256
    sizes = [tiles // 3 + (1 if k < tiles % 3 else 0) for k in range(3)]
    out, start = [], 0
    for s in sizes:
        if s:
            out.append((start * 256, s * 256))
            start += s
    return out


def _merge_ffn_kernel(*refs):
    merge_refs, (g_ref, wg_ref, wu_ref, wd_ref, out_ref) = refs[:8], refs[8:]
    x = _merge_body(*merge_refs)
    hn = _rms(x, g_ref[...]).astype(BF16)
    acc = x
    for c0, width in _ffn_chunks(wg_ref.shape[1]):
        gate = jnp.dot(hn, wg_ref[:, c0:c0 + width], preferred_element_type=F32)
        up = jnp.dot(hn, wu_ref[:, c0:c0 + width], preferred_element_type=F32)
        act = (gate * _sigmoid(gate) * up).astype(BF16)
        acc = acc + jnp.dot(act, wd_ref[c0:c0 + width, :], preferred_element_type=F32)
    out_ref[...] = acc


def _merge_ffn(merge_args, g, wg, wu, wd):
    tokens = merge_args[0].shape[0]
    d_ff = wg.shape[1]
    return pl.pallas_call(
        _merge_ffn_kernel,
        grid=(tokens // TM,),
        in_specs=_merge_specs() + [
            _resident((1, D_MODEL), lambda i: (0, 0)),
            _resident((D_MODEL, d_ff), lambda i: (0, 0)),
            _resident((D_MODEL, d_ff), lambda i: (0, 0)),
            _resident((d_ff, D_MODEL), lambda i: (0, 0)),
        ],
        out_specs=pl.BlockSpec((TM, D_MODEL), lambda i: (i, 0)),
        out_shape=jax.ShapeDtypeStruct((tokens, D_MODEL), F32),
        compiler_params=pltpu.CompilerParams(dimension_semantics=("parallel",), vmem_limit_bytes=FUSED_VMEM_LIMIT),
        name="merge_ffn",
    )(*merge_args, g, wg, wu, wd)


ROW_TILE = D_MODEL // LANES
assert ROW_TILE == SUBLANES


def _store_row_tiles(ref, index, rows, val):
    for s in range(ROW_TILE):
        ref[(*index, pl.ds(s, rows, stride=ROW_TILE), slice(None))] = val[:, s * LANES:(s + 1) * LANES]


def _row_tiles(row, n):
    start = row * ROW_TILE
    return pl.ds(start if isinstance(row, int) else pl.multiple_of(start, ROW_TILE), n * ROW_TILE)


def _load_row_tiles(ref, index, rows):
    return jnp.concatenate([ref[(*index, pl.ds(s, rows, stride=ROW_TILE), slice(None))]
                            for s in range(ROW_TILE)], axis=1)


def _router_kernel(x_ref, g_ref, whi_ref, wlo_ref, tri_ref, hp_ref, wc_ref, meta_ref, cnt_ref, carry_scr):
    i = pl.program_id(0)

    @pl.when(i == 0)
    def _():
        carry_scr[...] = jnp.zeros_like(carry_scr)

    hn = _rms(x_ref[...], g_ref[...])
    hi = hn.astype(BF16)
    lo = (hn - hi.astype(F32)).astype(BF16)
    _store_row_tiles(hp_ref, (), hn.shape[0], hn)

    whi = whi_ref[...]
    logits = (jnp.dot(hi, whi, preferred_element_type=F32) + jnp.dot(lo, whi, preferred_element_type=F32)
              + jnp.dot(hi, wlo_ref[...], preferred_element_type=F32))
    tm = logits.shape[0]
    lane = lax.broadcasted_iota(jnp.int32, (tm, LANES), 1).astype(F32)
    lg = jnp.where(lane < N_EXPERTS, logits, -jnp.inf)
    m1 = jnp.max(lg, axis=1, keepdims=True)
    i1 = jnp.min(jnp.where(lg == m1, lane, float(LANES)), axis=1, keepdims=True)
    lg2 = jnp.where(lane == i1, -jnp.inf, lg)
    m2 = jnp.max(lg2, axis=1, keepdims=True)
    i2 = jnp.min(jnp.where(lg2 == m2, lane, float(LANES)), axis=1, keepdims=True)
    e = jnp.exp(m2 - m1)
    w1 = 1.0 / (1.0 + e)
    w2 = e * w1
    wc_ref[...] = jnp.where(lane == 0.0, w1, jnp.where(lane == 1.0, w2, 0.0))

    hot1 = lane == i1
    hot2 = lane == i2
    onehot = jnp.where(hot1 | hot2, 1.0, 0.0).astype(BF16)
    cum = jnp.dot(tri_ref[...], onehot, preferred_element_type=F32)
    tot = carry_scr[...] + cum
    r1 = jnp.sum(jnp.where(hot1, tot - 1.0, 0.0), axis=1, keepdims=True)
    r2 = jnp.sum(jnp.where(hot2, tot - 1.0, 0.0), axis=1, keepdims=True)
    carry_scr[...] = tot[tm - 1:tm, :]
    cnt_ref[...] = tot[tm - 1:tm, :].astype(jnp.int32)

    cols = jnp.where(lane == 0.0, i1, jnp.where(lane == 1.0, i2, jnp.where(lane == 2.0, r1,
                     jnp.where(lane == 3.0, r2, 0.0))))
    meta_ref[...] = cols.T[:SUBLANES, :].astype(jnp.int32)


def _router(x2, g, whi, wlo):
    tokens = x2.shape[0]
    tri = jnp.tril(jnp.ones((TM, TM), F32)).astype(BF16)
    return pl.pallas_call(
        _router_kernel,
        grid=(tokens // TM,),
        in_specs=[
            pl.BlockSpec((TM, D_MODEL), lambda i: (i, 0)),
            _resident((1, D_MODEL), lambda i: (0, 0)),
            _resident((D_MODEL, LANES), lambda i: (0, 0)),
            _resident((D_MODEL, LANES), lambda i: (0, 0)),
            _resident((TM, TM), lambda i: (0, 0)),
        ],
        out_specs=[
            pl.BlockSpec((TM * ROW_TILE, LANES), lambda i: (i, 0)),
            pl.BlockSpec((TM, LANES), lambda i: (i, 0)),
            pl.BlockSpec((SUBLANES, TM), lambda i: (0, i)),
            pl.BlockSpec((1, LANES), lambda i: (0, 0)),
        ],
        out_shape=[
            jax.ShapeDtypeStruct((tokens * ROW_TILE, LANES), F32),
            jax.ShapeDtypeStruct((tokens, LANES), F32),
            jax.ShapeDtypeStruct((SUBLANES, tokens), jnp.int32),
            jax.ShapeDtypeStruct((1, LANES), jnp.int32),
        ],
        scratch_shapes=[pltpu.VMEM((1, LANES), F32)],
        compiler_params=_cparams(("arbitrary",)),
        name="router",
    )(x2, g, whi, wlo, tri)


def _dispatch_kernel(pad_ref, pos_ref, prev_ref, hp_ref, xs_ref, stage, zeros, sems, zsem):
    i = pl.program_id(0)
    n = pl.num_programs(0)
    tm = pos_ref.shape[1]

    @pl.when(i == 0)
    def _():
        zeros[...] = jnp.zeros_like(zeros)
        fill_rows = zeros.shape[0] // ROW_TILE
        used = pad_ref[N_EXPERTS]
        total = xs_ref.shape[0] // ROW_TILE - fill_rows

        def fill(start):
            return pltpu.make_async_copy(zeros, xs_ref.at[_row_tiles(start, fill_rows)], zsem)

        def tail(t):
            return pltpu.make_async_copy(zeros.at[_row_tiles(0, MOE_TM)], xs_ref.at[_row_tiles(t * MOE_TM, MOE_TM)],
                                         zsem)

        def run(cp):
            cp.start()
            cp.wait()

        for e in range(N_EXPERTS):
            run(fill(pad_ref[e]))
        run(fill(total))
        for t in range(total // MOE_TM - N_EXPERTS, total // MOE_TM):
            pl.when(t * MOE_TM >= used)(functools.partial(run, tail(t)))

    def copy(tile, idx_ref, r, k):
        slot = tile % 2
        return pltpu.make_async_copy(stage.at[slot, _row_tiles(r, 1)], xs_ref.at[_row_tiles(idx_ref[k, r], 1)],
                                     sems.at[slot])

    stage[i % 2] = hp_ref[...]

    def start(r, _):
        copy(i, pos_ref, r, 0).start(priority=0)
        copy(i, pos_ref, r, 1).start(priority=1)
        return 0

    def wait_tile(tile, idx_ref):
        def wait(r, _):
            copy(tile, idx_ref, r, 0).wait()
            copy(tile, idx_ref, r, 1).wait()
            return 0
        lax.fori_loop(0, tm, wait, 0, unroll=8)

    lax.fori_loop(0, tm, start, 0, unroll=8)

    @pl.when(i > 0)
    def _():
        wait_tile(i - 1, prev_ref)

    @pl.when(i == n - 1)
    def _():
        wait_tile(i, pos_ref)


FILL_ROWS = MOE_TM + SUBLANES


def _dispatch(pad_start8, pos3, hp, rows):
    tokens = hp.shape[0] // ROW_TILE
    return pl.pallas_call(
        _dispatch_kernel,
        grid=(tokens // GATHER_TM,),
        in_specs=[
            pl.BlockSpec(memory_space=pltpu.SMEM),
            pl.BlockSpec((None, TOP_K, GATHER_TM), lambda i: (i, 0, 0), memory_space=pltpu.SMEM),
            pl.BlockSpec((None, TOP_K, GATHER_TM), lambda i: (jnp.maximum(i - 1, 0), 0, 0),
                         memory_space=pltpu.SMEM),
            pl.BlockSpec((GATHER_TM * ROW_TILE, LANES), lambda i: (i, 0)),
        ],
        out_specs=pl.BlockSpec(memory_space=pl.ANY),
        out_shape=jax.ShapeDtypeStruct(((rows + FILL_ROWS) * ROW_TILE, LANES), F32),
        scratch_shapes=[pltpu.VMEM((2, GATHER_TM * ROW_TILE, LANES), F32),
                        pltpu.VMEM((FILL_ROWS * ROW_TILE, LANES), F32),
                        pltpu.SemaphoreType.DMA((2,)), pltpu.SemaphoreType.DMA],
        compiler_params=_cparams(("arbitrary",)),
        name="dispatch",
    )(pad_start8, pos3, pos3, hp)


def _moe_kernel(te_ref, tv_ref, ts_ref, xs_ref, wg_ref, wu_ref, wd_ref, ys_ref):
    del ts_ref
    i = pl.program_id(0)

    @pl.when(tv_ref[i] == 0)
    def _():
        ys_ref[...] = jnp.zeros_like(ys_ref)

    @pl.when(tv_ref[i] != 0)
    def _():
        x = _load_row_tiles(xs_ref, (), MOE_TM).astype(BF16)
        acc = jnp.zeros((MOE_TM, D_MODEL), F32)
        for c in range(wg_ref.shape[1] // MOE_FC):
            cs = slice(c * MOE_FC, (c + 1) * MOE_FC)
            gate = jnp.dot(x, wg_ref[:, cs], preferred_element_type=F32)
            up = jnp.dot(x, wu_ref[:, cs], preferred_element_type=F32)
            act = (gate * _sigmoid(gate) * up).astype(BF16)
            acc = acc + jnp.dot(act, wd_ref[cs, :], preferred_element_type=F32)
        _store_row_tiles(ys_ref, (), MOE_TM, acc)


def _moe(tile_expert, tile_valid, tile_src, xs, wg, wu, wd, rows):
    d_exp = wg.shape[2]
    wspec = lambda shape: pl.BlockSpec(shape, lambda i, te, tv, ts: (te[i], 0, 0), pipeline_mode=pl.Buffered(1))
    grid_spec = pltpu.PrefetchScalarGridSpec(
        num_scalar_prefetch=3,
        grid=(rows // MOE_TM,),
        in_specs=[
            pl.BlockSpec((MOE_TM * ROW_TILE, LANES), lambda i, te, tv, ts: (ts[i], 0)),
            wspec((None, D_MODEL, d_exp)), wspec((None, D_MODEL, d_exp)), wspec((None, d_exp, D_MODEL)),
        ],
        out_specs=pl.BlockSpec((MOE_TM * ROW_TILE, LANES), lambda i, te, tv, ts: (i, 0)),
    )
    return pl.pallas_call(
        _moe_kernel,
        grid_spec=grid_spec,
        out_shape=jax.ShapeDtypeStruct((rows * ROW_TILE, LANES), F32),
        compiler_params=_cparams(("arbitrary",)),
        name="experts",
    )(tile_expert, tile_valid, tile_src, xs, wg, wu, wd)


def _combine_kernel(pos_ref, next_ref, x_ref, wc_ref, g_ref, ys_ref, out_ref, buf, sems):
    i = pl.program_id(0)
    n = pl.num_programs(0)
    tm = x_ref.shape[0]

    def copy(slot, idx_ref, r, k):
        return pltpu.make_async_copy(ys_ref.at[_row_tiles(idx_ref[k, r], 1)], buf.at[slot, k, _row_tiles(r, 1)],
                                     sems.at[slot])

    def start_tile(slot, idx_ref):
        def start(r, _):
            copy(slot, idx_ref, r, 0).start(priority=0)
            copy(slot, idx_ref, r, 1).start(priority=1)
            return 0
        lax.fori_loop(0, tm, start, 0, unroll=8)

    slot = i % 2

    @pl.when(i == 0)
    def _():
        start_tile(0, pos_ref)

    @pl.when(i + 1 < n)
    def _():
        start_tile(1 - slot, next_ref)

    def wait(r, _):
        copy(slot, pos_ref, r, 0).wait()
        copy(slot, pos_ref, r, 1).wait()
        return 0

    lax.fori_loop(0, tm, wait, 0, unroll=8)
    wc = wc_ref[...]
    y0 = _load_row_tiles(buf, (slot, 0), tm)
    y1 = _load_row_tiles(buf, (slot, 1), tm)
    y = x_ref[...] + (wc[:, 0:1] * y0 + wc[:, 1:2] * y1)
    out_ref[...] = _rms(y, g_ref[...])


def _combine(pos3, x2, wcols, g, ys):
    tokens = x2.shape[0]
    ntiles = tokens // GATHER_TM
    return pl.pallas_call(
        _combine_kernel,
        grid=(tokens // GATHER_TM,),
        in_specs=[
            pl.BlockSpec((None, TOP_K, GATHER_TM), lambda i: (i, 0, 0), memory_space=pltpu.SMEM),
            pl.BlockSpec((None, TOP_K, GATHER_TM), lambda i: (jnp.minimum(i + 1, ntiles - 1), 0, 0),
                         memory_space=pltpu.SMEM),
            pl.BlockSpec((GATHER_TM, D_MODEL), lambda i: (i, 0)),
            pl.BlockSpec((GATHER_TM, LANES), lambda i: (i, 0)),
            _resident((1, D_MODEL), lambda i: (0, 0)),
            pl.BlockSpec(memory_space=pl.ANY),
        ],
        out_specs=pl.BlockSpec((GATHER_TM, D_MODEL), lambda i: (i, 0)),
        out_shape=jax.ShapeDtypeStruct((tokens, D_MODEL), F32),
        scratch_shapes=[pltpu.VMEM((2, TOP_K, GATHER_TM * ROW_TILE, LANES), F32), pltpu.SemaphoreType.DMA((2,))],
        compiler_params=_cparams(("arbitrary",)),
        name="combine",
    )(pos3, pos3, x2, wcols, g, ys)


def _rotary_lane_tables(seq):
    half = ROT_DIM // 2
    pos = jnp.arange(seq, dtype=F32)
    inv_freq = ROPE_THETA ** (-jnp.arange(0, ROT_DIM, 2, dtype=F32) / ROT_DIM)
    ang = pos[:, None] * inv_freq[None, :]
    cos, sin = jnp.cos(ang), jnp.sin(ang)
    j = np.arange(LANES) % HEAD_DIM
    idx = j % half
    cosf = jnp.where((j < ROT_DIM)[None, :], cos[:, idx], 1.0)
    sina = jnp.where(((j >= half) & (j < ROT_DIM))[None, :], sin[:, idx], 0.0)
    sinb = jnp.where((j < half)[None, :], -sin[:, idx], 0.0)
    return cosf, sina, sinb


def _mixer(x2, layer, p, tables, batch, seq):
    cuts = np.cumsum([D_RNN, D_RNN, ATT_WIDTH, ATT_WIDTH, ATT_WIDTH])
    w = p["w_in"][layer]
    w_perm = jnp.concatenate([w[:, :cuts[1]], w[:, cuts[4]:], w[:, cuts[1]:cuts[4]]], axis=1).astype(BF16)
    xc, yg, qkv4 = _in_proj(x2, p["norm_mix_g"][layer][None], w_perm, p["conv_w"][layer],
                            p["conv_b"][layer][None], *tables, batch, seq)

    wgate = (0.5 * jnp.concatenate([p["w_rgate"][layer], p["w_igate"][layer]], axis=-1)).astype(BF16)
    hf, hb = _lru(xc.reshape(batch, seq, D_RNN), wgate, 0.5 * p["b_rgate"][layer], 0.5 * p["b_igate"][layer],
                  p["lru_lambda"][layer])
    att = _attention(qkv4, batch, seq)

    tokens = batch * seq
    return (x2, hf.reshape(tokens, D_RNN), hb.reshape(tokens, D_RNN), yg, att,
            p["w_proj_lru"][layer].astype(BF16), p["w_proj_att"][layer].astype(BF16), p["w_out"][layer].astype(BF16))


def _moe_layer(merge_args, j, p, final_g):
    tokens = merge_args[0].shape[0]
    wr = jnp.pad(p["w_router"][j], ((0, 0), (0, LANES - N_EXPERTS)))
    whi = wr.astype(BF16)
    wlo = (wr - whi.astype(F32)).astype(BF16)
    x2 = _merge(merge_args)
    hp, wcols, meta, counts = _router(x2, p["norm_ffn_g"][j * 2 + 1][None], whi, wlo)

    counts = counts[0, :N_EXPERTS]
    padded = ((counts + MOE_TM - 1) // MOE_TM) * MOE_TM
    ends = jnp.cumsum(padded)
    starts = ends - padded
    experts_of = meta[0:TOP_K]
    base = jnp.zeros_like(experts_of)
    for e in range(N_EXPERTS):
        base = base + jnp.where(experts_of == e, starts[e], 0)
    pos = base + meta[TOP_K:2 * TOP_K]
    pos3 = pos.reshape(TOP_K, tokens // GATHER_TM, GATHER_TM).transpose(1, 0, 2)

    rows = TOP_K * tokens + N_EXPERTS * MOE_TM
    tile_start = jnp.arange(rows // MOE_TM, dtype=jnp.int32) * MOE_TM
    tile_expert = jnp.minimum(jnp.sum(tile_start[:, None] >= ends[None, :], axis=1), N_EXPERTS - 1).astype(jnp.int32)
    tile_valid = (tile_start < ends[-1]).astype(jnp.int32)
    tile_src = jnp.minimum(tile_start // MOE_TM, ends[-1] // MOE_TM - 1).astype(jnp.int32)
    pad_start8 = ((starts + counts) // SUBLANES * SUBLANES).astype(jnp.int32)
    pad_start8 = jnp.concatenate([pad_start8, ends[-1:].astype(jnp.int32)])

    xs = _dispatch(pad_start8, pos3, hp, rows)
    ys = _moe(tile_expert, tile_valid, tile_src, xs, p["w_exp_gate"][j].astype(BF16),
              p["w_exp_up"][j].astype(BF16), p["w_exp_down"][j].astype(BF16), rows)
    return _combine(pos3, x2, wcols, final_g[None], ys)


def kernel(x, norm_mix_g, w_in, conv_w, conv_b, w_rgate, b_rgate, w_igate, b_igate, lru_lambda, w_proj_lru,
           w_proj_att, w_out, norm_ffn_g, w_dense_gate, w_dense_up, w_dense_down, w_router, w_exp_gate, w_exp_up,
           w_exp_down, final_norm_g):
    batch, seq, _ = x.shape
    depth = w_in.shape[0]
    assert depth % 2 == 0, "the final RMSNorm is fused into the last (routed) layer"
    p = dict(norm_mix_g=norm_mix_g, w_in=w_in, conv_w=conv_w, conv_b=conv_b, w_rgate=w_rgate, b_rgate=b_rgate,
             w_igate=w_igate, b_igate=b_igate, lru_lambda=lru_lambda, w_proj_lru=w_proj_lru,
             w_proj_att=w_proj_att, w_out=w_out, norm_ffn_g=norm_ffn_g, w_router=w_router,
             w_exp_gate=w_exp_gate, w_exp_up=w_exp_up, w_exp_down=w_exp_down)
    tables = _rotary_lane_tables(seq)
    x2 = x.reshape(batch * seq, D_MODEL)
    for layer in range(depth):
        merge_args = _mixer(x2, layer, p, tables, batch, seq)
        j = layer // 2
        if layer % 2 == 0:
            x2 = _merge_ffn(merge_args, norm_ffn_g[layer][None], w_dense_gate[j].astype(BF16),
                            w_dense_up[j].astype(BF16), w_dense_down[j].astype(BF16))
        else:
            assert layer == depth - 1
            x2 = _moe_layer(merge_args, j, p, final_norm_g)
    return x2.reshape(batch, seq, D_MODEL)
```

```python
import functools

import jax
import jax.numpy as jnp
import numpy as np
from jax import lax
from jax.experimental import pallas as pl
from jax.experimental.pallas import tpu as pltpu

D_MODEL = 1024
D_RNN = D_MODEL
N_LRU_BLOCKS = 8
LRU_BLOCK = D_RNN // N_LRU_BLOCKS
CONV_WIDTH = 4
LRU_C = 8.0
HEAD_DIM = 64
HEADS_PER_GROUP = 4
WINDOWS = (128, 512, 2048)
DILATIONS = (1, 4, 16)
N_GROUPS = len(WINDOWS)
ATT_WIDTH = N_GROUPS * HEADS_PER_GROUP * HEAD_DIM
ATT_OUT = HEADS_PER_GROUP * HEAD_DIM
ROT_DIM = HEAD_DIM // 4
ROPE_THETA = 500000.0
HALF_KEYS = 64
IN_WIDTH = 2 * D_RNN + 3 * ATT_WIDTH + 2 * D_MODEL
QKV_WIDTH = 3 * ATT_WIDTH
YG_WIDTH = 3 * D_MODEL
N_EXPERTS = 8
TOP_K = 2
RMS_EPS = 1e-6

LANES = 128
SUBLANES = 8
VMEM_LIMIT = 52 * 1024 * 1024
FUSED_VMEM_LIMIT = 58 * 1024 * 1024

TM = 512
LRU_TS = 256
LRU_CB = 256
ATT_CQ = 128
ATT_KW = ATT_CQ + 2 * HALF_KEYS
ATT_UNROLL_ROWS = 8 * ATT_CQ
MOE_TM = 512
MOE_FC = 512
GATHER_TM = 256

BF16 = jnp.bfloat16
F32 = jnp.float32
LOG2E = float(np.log2(np.e))
LN2 = float(np.log(2.0))
Q_SCALE = HEAD_DIM ** -0.5 * LOG2E
SQRT_FLOOR = 1e-30


def _cparams(sem):
    return pltpu.CompilerParams(dimension_semantics=sem, vmem_limit_bytes=VMEM_LIMIT)


def _resident(shape, index_map):
    return pl.BlockSpec(shape, index_map, pipeline_mode=pl.Buffered(1))


def _sigmoid(x):
    return 0.5 * jnp.tanh(0.5 * x) + 0.5


def _rms(x, g):
    ms = jnp.mean(x * x, axis=-1, keepdims=True)
    return x * lax.rsqrt(ms + RMS_EPS) * g


def _in_proj_kernel(seq_tiles, x_ref, xp_ref, xn_ref, g_ref, w_ref, cw_ref, cb_ref, cos_ref, sa_ref, sb_ref,
                    xc_ref, yg_ref, qkv_ref):
    i = pl.program_id(0)
    tm = x_ref.shape[0]
    halo = xp_ref.shape[0]
    xp = jnp.where((i % seq_tiles) == 0, 0.0, xp_ref[...])
    xn = jnp.where((i % seq_tiles) == seq_tiles - 1, 0.0, xn_ref[...])
    hn_ext = _rms(jnp.concatenate([xp, x_ref[...], xn], axis=0), g_ref[...]).astype(BF16)
    hn = hn_ext[halo:halo + tm]

    def proj(c0, width):
        return jnp.dot(hn, w_ref[:, c0:c0 + width], preferred_element_type=F32)

    cw = cw_ref[...]
    for c in range(D_RNN // 512):
        cs = slice(c * 512, (c + 1) * 512)
        ext = jnp.dot(hn_ext, w_ref[:, cs], preferred_element_type=F32)
        xc = cb_ref[:, cs]
        for j in range(CONV_WIDTH):
            xc = xc + ext[halo - 1 + j:halo - 1 + j + tm] * cw[j:j + 1, cs]
        xc_ref[:, cs] = xc
    for c in range(YG_WIDTH // 512):
        yg_ref[:, c * 512:(c + 1) * 512] = proj(D_RNN + c * 512, 512).astype(BF16)

    cosf, sina, sinb = cos_ref[...], sa_ref[...], sb_ref[...]
    base = D_RNN + YG_WIDTH
    for c in range(QKV_WIDTH // 256):
        acc = proj(base + c * 256, 256)
        if c < 2 * ATT_WIDTH // 256:
            for h in range(2):
                t = acc[:, h * LANES:(h + 1) * LANES]
                r = t * cosf + pltpu.roll(t, 8, 1) * sina + pltpu.roll(t, LANES - 8, 1) * sinb
                qkv_ref[2 * c + h] = r * Q_SCALE if c < ATT_WIDTH // 256 else r
        else:
            for h in range(2):
                qkv_ref[2 * c + h] = acc[:, h * LANES:(h + 1) * LANES]


def _in_proj(x2, g, w_perm, conv_w, conv_b, cosf, sina, sinb, batch, seq):
    tokens = batch * seq
    ns = seq // TM
    halo_blocks = TM // SUBLANES
    return pl.pallas_call(
        functools.partial(_in_proj_kernel, ns),
        grid=(tokens // TM,),
        in_specs=[
            pl.BlockSpec((TM, D_MODEL), lambda i: (i, 0)),
            pl.BlockSpec((SUBLANES, D_MODEL), lambda i: (jnp.maximum(i * halo_blocks - 1, 0), 0)),
            pl.BlockSpec((SUBLANES, D_MODEL),
                         lambda i: (jnp.minimum((i + 1) * halo_blocks, tokens // SUBLANES - 1), 0)),
            _resident((1, D_MODEL), lambda i: (0, 0)),
            _resident((D_MODEL, IN_WIDTH), lambda i: (0, 0)),
            _resident((CONV_WIDTH, D_RNN), lambda i: (0, 0)),
            _resident((1, D_RNN), lambda i: (0, 0)),
            pl.BlockSpec((TM, LANES), lambda i: (i % ns, 0)),
            pl.BlockSpec((TM, LANES), lambda i: (i % ns, 0)),
            pl.BlockSpec((TM, LANES), lambda i: (i % ns, 0)),
        ],
        out_specs=[
            pl.BlockSpec((TM, D_RNN), lambda i: (i, 0)),
            pl.BlockSpec((TM, YG_WIDTH), lambda i: (i, 0)),
            pl.BlockSpec((None, QKV_WIDTH // LANES, TM, LANES), lambda i: (i // ns, 0, i % ns, 0)),
        ],
        out_shape=[
            jax.ShapeDtypeStruct((tokens, D_RNN), F32),
            jax.ShapeDtypeStruct((tokens, YG_WIDTH), BF16),
            jax.ShapeDtypeStruct((batch, QKV_WIDTH // LANES, seq, LANES), F32),
        ],
        compiler_params=_cparams(("parallel",)),
        name="in_proj",
    )(x2, x2, x2, g, w_perm, conv_w, conv_b, cosf, sina, sinb)


def _lru_kernel(x_hbm, wg_ref, br_ref, bi_ref, lam_ref, hf_ref, hb_ref, xbuf, a_scr, u_scr, h_scr, carry_scr, sems):
    n, c = pl.program_id(0), pl.program_id(1)
    nc = pl.num_programs(1)
    step_id = n * nc + c
    last_step = pl.num_programs(0) * nc - 1
    nb, ts = hf_ref.shape[0], hf_ref.shape[1]

    @pl.when(c == 0)
    def _():
        carry_scr[...] = jnp.zeros_like(carry_scr)

    def copies(sn, sc, slot):
        col = pl.ds(pl.multiple_of(sn * LRU_CB, LRU_CB), LRU_CB)
        out = []
        for direction in range(2):
            chunk = sc if direction == 0 else nc - 1 - sc
            rows = pl.ds(pl.multiple_of(chunk * ts, ts), ts)
            out += [pltpu.make_async_copy(x_hbm.at[b, rows, col], xbuf.at[slot, direction, :, b, :],
                                          sems.at[slot, direction]) for b in range(nb)]
        return out

    slot = step_id % 2

    @pl.when(step_id == 0)
    def _():
        for cp in copies(n, c, 0):
            cp.start()

    @pl.when(step_id < last_step)
    def _():
        nxt = step_id + 1
        for cp in copies(nxt // nc, nxt % nc, 1 - slot):
            cp.start()

    for cp in copies(n, c, slot):
        cp.wait()

    def gates(direction):
        xc = xbuf[slot, direction]
        for j in range(LRU_CB // LRU_BLOCK):
            sl = slice(j * LRU_BLOCK, (j + 1) * LRU_BLOCK)
            xcb = xc[:, :, sl].reshape(ts * nb, LRU_BLOCK)
            rg = jnp.dot(xcb.astype(BF16), wg_ref[direction, j], preferred_element_type=F32)
            tr = jnp.tanh(rg[:, :LRU_BLOCK] + br_ref[direction:direction + 1, sl])
            ti = jnp.tanh(rg[:, LRU_BLOCK:] + bi_ref[direction:direction + 1, sl])
            lam = lam_ref[direction:direction + 1, sl]
            c2 = (-0.5 * LRU_C * LOG2E) * (jnp.maximum(-lam, 0.0) + jnp.log(1.0 + jnp.exp(-jnp.abs(lam))))
            a = jnp.exp2(tr * c2 + c2)
            xh = 0.5 * xcb
            gated = ti * xh + xh
            om = 1.0 - a * a
            u = (om * lax.rsqrt(jnp.maximum(om, SQRT_FLOOR))) * gated
            a_scr[direction, :, :, sl] = a.reshape(ts, nb, LRU_BLOCK)
            u_scr[direction, :, :, sl] = u.reshape(ts, nb, LRU_BLOCK)

    gates(0)
    gates(1)

    def step(t, carry):
        hf, hb = carry
        tb = ts - 1 - t
        hf = a_scr[0, t] * hf + u_scr[0, t]
        hb = a_scr[1, tb] * hb + u_scr[1, tb]
        h_scr[0, t] = hf
        h_scr[1, tb] = hb
        return hf, hb

    hf, hb = lax.fori_loop(0, ts, step, (carry_scr[0], carry_scr[1]), unroll=8)
    carry_scr[0] = hf
    carry_scr[1] = hb
    hf_ref[...] = jnp.swapaxes(h_scr[0], 0, 1).astype(hf_ref.dtype)
    hb_ref[...] = jnp.swapaxes(h_scr[1], 0, 1).astype(hb_ref.dtype)


def _lru(xc3, wgate, b_r, b_i, lam):
    batch, seq, _ = xc3.shape
    nc = seq // LRU_TS
    ncb = D_RNN // LRU_CB

    def main(rev):
        return pl.BlockSpec((batch, LRU_TS, LRU_CB), lambda n, c: (0, (nc - 1 - c) if rev else c, n))

    vec2 = pl.BlockSpec((2, LRU_CB), lambda n, c: (0, n))
    out_sds = jax.ShapeDtypeStruct((batch, seq, D_RNN), BF16)
    return pl.pallas_call(
        _lru_kernel,
        grid=(ncb, nc),
        in_specs=[
            pl.BlockSpec(memory_space=pl.ANY),
            pl.BlockSpec((2, LRU_CB // LRU_BLOCK, LRU_BLOCK, 2 * LRU_BLOCK), lambda n, c: (0, n, 0, 0)),
            vec2, vec2, vec2,
        ],
        out_specs=[main(False), main(True)],
        out_shape=[out_sds, out_sds],
        scratch_shapes=[
            pltpu.VMEM((2, 2, LRU_TS, batch, LRU_CB), F32),
            pltpu.VMEM((2, LRU_TS, batch, LRU_CB), F32),
            pltpu.VMEM((2, LRU_TS, batch, LRU_CB), F32),
            pltpu.VMEM((2, LRU_TS, batch, LRU_CB), F32),
            pltpu.VMEM((2, batch, LRU_CB), F32),
            pltpu.SemaphoreType.DMA((2, 2)),
        ],
        compiler_params=_cparams(("arbitrary", "arbitrary")),
        name="lru",
    )(xc3, wgate, b_r, b_i, lam)


def _attn_kernel(q_ref, k_ref, v_ref, bias_ref, wide_bias_ref, out_ref, acc_o, acc_l):
    g = pl.program_id(1)
    seq = q_ref.shape[1]
    nblk = ATT_OUT // LANES
    lane_head = lax.broadcasted_iota(jnp.int32, (1, ATT_OUT), 1) // HEAD_DIM

    def rows(start, n, d):
        return pl.ds(start, n) if d == 1 else pl.ds(start, n, stride=d)

    def load(ref, start, n, d):
        return jnp.concatenate([ref[j, rows(start, n, d), :] for j in range(nblk)], axis=1)

    def store(ref, start, n, d, val):
        for j in range(nblk):
            ref[j, rows(start, n, d), :] = val[:, j * LANES:(j + 1) * LANES]

    def run_group(gi):
        d = DILATIONS[gi]
        length = seq // d
        cq = length if length == ATT_KW else ATT_CQ
        chunks_per_residue = length // cq

        def chunk(n, _):
            r = n // chunks_per_residue
            qs = (n - r * chunks_per_residue) * cq
            ks = jnp.clip(qs - HALF_KEYS, 0, length - ATT_KW)
            q0, k0 = r + d * qs, r + d * ks
            if d == 1:
                q0, k0 = pl.multiple_of(q0, ATT_CQ), pl.multiple_of(k0, HALF_KEYS)
            q = load(q_ref, q0, cq, d).astype(BF16)
            kk = load(k_ref, k0, ATT_KW, d).astype(BF16)
            vv = load(v_ref, k0, ATT_KW, d).astype(BF16)
            q4 = jnp.concatenate([jnp.where(lane_head == h, q, jnp.zeros_like(q))
                                  for h in range(HEADS_PER_GROUP)], axis=0)
            s = lax.dot_general(q4, kk, (((1,), (1,)), ((), ())), preferred_element_type=F32)
            s = s + (bias_ref[(qs - ks) // HALF_KEYS] if cq == ATT_CQ else wide_bias_ref[...])
            m = jnp.max(s, axis=1, keepdims=True)
            p = jnp.exp2(s - m)
            l = jnp.sum(p, axis=1, keepdims=True)
            pv = jnp.dot(p.astype(BF16), vv, preferred_element_type=F32)
            inv = 1.0 / l
            lse = m * LN2 + jnp.log(l)
            o = jnp.zeros((cq, ATT_OUT), F32)
            lf = jnp.zeros((cq, ATT_OUT), F32)
            for h in range(HEADS_PER_GROUP):
                sl = slice(h * cq, (h + 1) * cq)
                sel = lane_head == h
                o = jnp.where(sel, pv[sl] * inv[sl], o)
                lf = jnp.where(sel, lse[sl], lf)
            if gi == 0:
                store(acc_o, q0, cq, d, o)
                store(acc_l, q0, cq, d, lf)
            else:
                ao = load(acc_o, q0, cq, d)
                al = load(acc_l, q0, cq, d)
                top = jnp.maximum(al, lf)
                wa, wg = jnp.exp(al - top), jnp.exp(lf - top)
                den = wa + wg
                store(acc_o, q0, cq, d, (wa * ao + wg * o) * (1.0 / den))
                if gi < N_GROUPS - 1:
                    store(acc_l, q0, cq, d, top + jnp.log(den))
            return 0

        lax.fori_loop(0, seq // cq, chunk, 0, unroll=ATT_UNROLL_ROWS // cq)

    for gi in range(N_GROUPS):
        pl.when(g == gi)(functools.partial(run_group, gi))

    @pl.when(g == N_GROUPS - 1)
    def _():
        out_ref[...] = jnp.concatenate([acc_o[j] for j in range(nblk)], axis=1).astype(out_ref.dtype)


def _attention(qkv4, batch, seq):
    per_kind = ATT_WIDTH // ATT_OUT
    nblk = ATT_OUT // LANES

    def spec(kind):
        return pl.BlockSpec((None, nblk, seq, LANES), lambda b, g: (b, kind * per_kind + g, 0, 0))

    rows4 = HEADS_PER_GROUP * ATT_CQ
    qrow = (np.arange(rows4) % ATT_CQ)[None, :, None]
    kcol = np.arange(ATT_KW)[None, None, :]
    shift = (np.arange(3) * HALF_KEYS)[:, None, None]
    bias = jnp.asarray(np.where(np.abs(kcol - shift - qrow) <= HALF_KEYS, 0.0, -np.inf), F32)
    wide_rows = HEADS_PER_GROUP * ATT_KW
    wide_q = (np.arange(wide_rows) % ATT_KW)[:, None]
    wide_bias = jnp.asarray(np.where(np.abs(np.arange(ATT_KW)[None, :] - wide_q) <= HALF_KEYS, 0.0, -np.inf), F32)

    return pl.pallas_call(
        _attn_kernel,
        grid=(batch, N_GROUPS),
        in_specs=[spec(0), spec(1), spec(2), _resident((3, rows4, ATT_KW), lambda b, g: (0, 0, 0)),
                  _resident((wide_rows, ATT_KW), lambda b, g: (0, 0))],
        out_specs=pl.BlockSpec((seq, ATT_OUT), lambda b, g: (b, 0)),
        out_shape=jax.ShapeDtypeStruct((batch * seq, ATT_OUT), BF16),
        scratch_shapes=[pltpu.VMEM((nblk, seq, LANES), F32), pltpu.VMEM((nblk, seq, LANES), F32)],
        compiler_params=_cparams(("parallel", "arbitrary")),
        name="attention",
    )(qkv4, qkv4, qkv4, bias, wide_bias)


def _merge_body(x_ref, hf_ref, hb_ref, yg_ref, att_ref, wpl_ref, wpa_ref, wo_ref):
    y = yg_ref[:, :D_MODEL].astype(F32)
    gelu = 0.5 * y * (1.0 + jnp.tanh(np.sqrt(2.0 / np.pi) * (y + 0.044715 * (y * y * y))))
    lru = ((hf_ref[...].astype(F32) + hb_ref[...].astype(F32)) * gelu).astype(BF16)
    branch_lru = jnp.dot(lru, wpl_ref[...], preferred_element_type=F32)
    branch_att = jnp.dot(att_ref[...], wpa_ref[...], preferred_element_type=F32)

    g_lru = _sigmoid(yg_ref[:, D_MODEL:2 * D_MODEL].astype(F32))
    g_att = _sigmoid(yg_ref[:, 2 * D_MODEL:].astype(F32))
    merged = (g_lru * branch_lru + g_att * branch_att).astype(BF16)
    return x_ref[...] + jnp.dot(merged, wo_ref[...], preferred_element_type=F32)


def _merge_specs():
    tok = lambda w: pl.BlockSpec((TM, w), lambda i: (i, 0))
    return [tok(D_MODEL), tok(D_RNN), tok(D_RNN), tok(YG_WIDTH), tok(ATT_OUT),
            _resident((D_RNN, D_MODEL), lambda i: (0, 0)),
            _resident((ATT_OUT, D_MODEL), lambda i: (0, 0)),
            _resident((D_MODEL, D_MODEL), lambda i: (0, 0))]


def _merge_kernel(*refs):
    refs[8][...] = _merge_body(*refs[:8])


def _merge(merge_args):
    tokens = merge_args[0].shape[0]
    return pl.pallas_call(
        _merge_kernel,
        grid=(tokens // TM,),
        in_specs=_merge_specs(),
        out_specs=pl.BlockSpec((TM, D_MODEL), lambda i: (i, 0)),
        out_shape=jax.ShapeDtypeStruct((tokens, D_MODEL), F32),
        compiler_params=_cparams(("parallel",)),
        name="merge",
    )(*merge_args)


def _ffn_chunks(d_ff):
    tiles = d_ff // 256
    sizes = [tiles // 3 + (1 if k < tiles % 3 else 0) for k in range(3)]
    out, start = [], 0
    for s in sizes:
        if s:
            out.append((start * 256, s * 256))
            start += s
    return out


def _merge_ffn_kernel(*refs):
    merge_refs, (g_ref, wg_ref, wu_ref, wd_ref, out_ref) = refs[:8], refs[8:]
    x = _merge_body(*merge_refs)
    hn = _rms(x, g_ref[...]).astype(BF16)
    acc = x
    for c0, width in _ffn_chunks(wg_ref.shape[1]):
        gate = jnp.dot(hn, wg_ref[:, c0:c0 + width], preferred_element_type=F32)
        up = jnp.dot(hn, wu_ref[:, c0:c0 + width], preferred_element_type=F32)
        act = (gate * _sigmoid(gate) * up).astype(BF16)
        acc = acc + jnp.dot(act, wd_ref[c0:c0 + width, :], preferred_element_type=F32)
    out_ref[...] = acc


def _merge_ffn(merge_args, g, wg, wu, wd):
    tokens = merge_args[0].shape[0]
    d_ff = wg.shape[1]
    return pl.pallas_call(
        _merge_ffn_kernel,
        grid=(tokens // TM,),
        in_specs=_merge_specs() + [
            _resident((1, D_MODEL), lambda i: (0, 0)),
            _resident((D_MODEL, d_ff), lambda i: (0, 0)),
            _resident((D_MODEL, d_ff), lambda i: (0, 0)),
            _resident((d_ff, D_MODEL), lambda i: (0, 0)),
        ],
        out_specs=pl.BlockSpec((TM, D_MODEL), lambda i: (i, 0)),
        out_shape=jax.ShapeDtypeStruct((tokens, D_MODEL), F32),
        compiler_params=pltpu.CompilerParams(dimension_semantics=("parallel",), vmem_limit_bytes=FUSED_VMEM_LIMIT),
        name="merge_ffn",
    )(*merge_args, g, wg, wu, wd)


ROW_TILE = D_MODEL // LANES
assert ROW_TILE == SUBLANES


def _store_row_tiles(ref, index, rows, val):
    for s in range(ROW_TILE):
        ref[(*index, pl.ds(s, rows, stride=ROW_TILE), slice(None))] = val[:, s * LANES:(s + 1) * LANES]


def _row_tiles(row, n):
    start = row * ROW_TILE
    return pl.ds(start if isinstance(row, int) else pl.multiple_of(start, ROW_TILE), n * ROW_TILE)


def _load_row_tiles(ref, index, rows):
    return jnp.concatenate([ref[(*index, pl.ds(s, rows, stride=ROW_TILE), slice(None))]
                            for s in range(ROW_TILE)], axis=1)


def _router_kernel(x_ref, g_ref, whi_ref, wlo_ref, tri_ref, hp_ref, wc_ref, meta_ref, cnt_ref, carry_scr):
    i = pl.program_id(0)

    @pl.when(i == 0)
    def _():
        carry_scr[...] = jnp.zeros_like(carry_scr)

    hn = _rms(x_ref[...], g_ref[...])
    hi = hn.astype(BF16)
    lo = (hn - hi.astype(F32)).astype(BF16)
    _store_row_tiles(hp_ref, (), hn.shape[0], hn)

    whi = whi_ref[...]
    logits = (jnp.dot(hi, whi, preferred_element_type=F32) + jnp.dot(lo, whi, preferred_element_type=F32)
              + jnp.dot(hi, wlo_ref[...], preferred_element_type=F32))
    tm = logits.shape[0]
    lane = lax.broadcasted_iota(jnp.int32, (tm, LANES), 1).astype(F32)
    lg = jnp.where(lane < N_EXPERTS, logits, -jnp.inf)
    m1 = jnp.max(lg, axis=1, keepdims=True)
    i1 = jnp.min(jnp.where(lg == m1, lane, float(LANES)), axis=1, keepdims=True)
    lg2 = jnp.where(lane == i1, -jnp.inf, lg)
    m2 = jnp.max(lg2, axis=1, keepdims=True)
    i2 = jnp.min(jnp.where(lg2 == m2, lane, float(LANES)), axis=1, keepdims=True)
    e = jnp.exp(m2 - m1)
    w1 = 1.0 / (1.0 + e)
    w2 = e * w1
    wc_ref[...] = jnp.where(lane == 0.0, w1, jnp.where(lane == 1.0, w2, 0.0))

    hot1 = lane == i1
    hot2 = lane == i2
    onehot = jnp.where(hot1 | hot2, 1.0, 0.0).astype(BF16)
    cum = jnp.dot(tri_ref[...], onehot, preferred_element_type=F32)
    tot = carry_scr[...] + cum
    r1 = jnp.sum(jnp.where(hot1, tot - 1.0, 0.0), axis=1, keepdims=True)
    r2 = jnp.sum(jnp.where(hot2, tot - 1.0, 0.0), axis=1, keepdims=True)
    carry_scr[...] = tot[tm - 1:tm, :]
    cnt_ref[...] = tot[tm - 1:tm, :].astype(jnp.int32)

    cols = jnp.where(lane == 0.0, i1, jnp.where(lane == 1.0, i2, jnp.where(lane == 2.0, r1,
                     jnp.where(lane == 3.0, r2, 0.0))))
    meta_ref[...] = cols.T[:SUBLANES, :].astype(jnp.int32)


def _router(x2, g, whi, wlo):
    tokens = x2.shape[0]
    tri = jnp.tril(jnp.ones((TM, TM), F32)).astype(BF16)
    return pl.pallas_call(
        _router_kernel,
        grid=(tokens // TM,),
        in_specs=[
            pl.BlockSpec((TM, D_MODEL), lambda i: (i, 0)),
            _resident((1, D_MODEL), lambda i: (0, 0)),
            _resident((D_MODEL, LANES), lambda i: (0, 0)),
            _resident((D_MODEL, LANES), lambda i: (0, 0)),
            _resident((TM, TM), lambda i: (0, 0)),
        ],
        out_specs=[
            pl.BlockSpec((TM * ROW_TILE, LANES), lambda i: (i, 0)),
            pl.BlockSpec((TM, LANES), lambda i: (i, 0)),
            pl.BlockSpec((SUBLANES, TM), lambda i: (0, i)),
            pl.BlockSpec((1, LANES), lambda i: (0, 0)),
        ],
        out_shape=[
            jax.ShapeDtypeStruct((tokens * ROW_TILE, LANES), F32),
            jax.ShapeDtypeStruct((tokens, LANES), F32),
            jax.ShapeDtypeStruct((SUBLANES, tokens), jnp.int32),
            jax.ShapeDtypeStruct((1, LANES), jnp.int32),
        ],
        scratch_shapes=[pltpu.VMEM((1, LANES), F32)],
        compiler_params=_cparams(("arbitrary",)),
        name="router",
    )(x2, g, whi, wlo, tri)


def _dispatch_kernel(pad_ref, pos_ref, prev_ref, hp_ref, xs_ref, stage, zeros, sems, zsem):
    i = pl.program_id(0)
    n = pl.num_programs(0)
    tm = pos_ref.shape[1]

    @pl.when(i == 0)
    def _():
        zeros[...] = jnp.zeros_like(zeros)
        fill_rows = zeros.shape[0] // ROW_TILE
        used = pad_ref[N_EXPERTS]
        total = xs_ref.shape[0] // ROW_TILE - fill_rows

        def fill(start):
            return pltpu.make_async_copy(zeros, xs_ref.at[_row_tiles(start, fill_rows)], zsem)

        def tail(t):
            return pltpu.make_async_copy(zeros.at[_row_tiles(0, MOE_TM)], xs_ref.at[_row_tiles(t * MOE_TM, MOE_TM)],
                                         zsem)

        def run(cp):
            cp.start()
            cp.wait()

        for e in range(N_EXPERTS):
            run(fill(pad_ref[e]))
        run(fill(total))
        for t in range(total // MOE_TM - N_EXPERTS, total // MOE_TM):
            pl.when(t * MOE_TM >= used)(functools.partial(run, tail(t)))

    def copy(tile, idx_ref, r, k):
        slot = tile % 2
        return pltpu.make_async_copy(stage.at[slot, _row_tiles(r, 1)], xs_ref.at[_row_tiles(idx_ref[k, r], 1)],
                                     sems.at[slot])

    stage[i % 2] = hp_ref[...]

    def start(r, _):
        copy(i, pos_ref, r, 0).start(priority=0)
        copy(i, pos_ref, r, 1).start(priority=1)
        return 0

    def wait_tile(tile, idx_ref):
        def wait(r, _):
            copy(tile, idx_ref, r, 0).wait()
            copy(tile, idx_ref, r, 1).wait()
            return 0
        lax.fori_loop(0, tm, wait, 0, unroll=8)

    lax.fori_loop(0, tm, start, 0, unroll=8)

    @pl.when(i > 0)
    def _():
        wait_tile(i - 1, prev_ref)

    @pl.when(i == n - 1)
    def _():
        wait_tile(i, pos_ref)


FILL_ROWS = MOE_TM + SUBLANES


def _dispatch(pad_start8, pos3, hp, rows):
    tokens = hp.shape[0] // ROW_TILE
    return pl.pallas_call(
        _dispatch_kernel,
        grid=(tokens // GATHER_TM,),
        in_specs=[
            pl.BlockSpec(memory_space=pltpu.SMEM),
            pl.BlockSpec((None, TOP_K, GATHER_TM), lambda i: (i, 0, 0), memory_space=pltpu.SMEM),
            pl.BlockSpec((None, TOP_K, GATHER_TM), lambda i: (jnp.maximum(i - 1, 0), 0, 0),
                         memory_space=pltpu.SMEM),
            pl.BlockSpec((GATHER_TM * ROW_TILE, LANES), lambda i: (i, 0)),
        ],
        out_specs=pl.BlockSpec(memory_space=pl.ANY),
        out_shape=jax.ShapeDtypeStruct(((rows + FILL_ROWS) * ROW_TILE, LANES), F32),
        scratch_shapes=[pltpu.VMEM((2, GATHER_TM * ROW_TILE, LANES), F32),
                        pltpu.VMEM((FILL_ROWS * ROW_TILE, LANES), F32),
                        pltpu.SemaphoreType.DMA((2,)), pltpu.SemaphoreType.DMA],
        compiler_params=_cparams(("arbitrary",)),
        name="dispatch",
    )(pad_start8, pos3, pos3, hp)


def _moe_kernel(te_ref, tv_ref, ts_ref, xs_ref, wg_ref, wu_ref, wd_ref, ys_ref):
    del ts_ref
    i = pl.program_id(0)

    @pl.when(tv_ref[i] == 0)
    def _():
        ys_ref[...] = jnp.zeros_like(ys_ref)

    @pl.when(tv_ref[i] != 0)
    def _():
        x = _load_row_tiles(xs_ref, (), MOE_TM).astype(BF16)
        acc = jnp.zeros((MOE_TM, D_MODEL), F32)
        for c in range(wg_ref.shape[1] // MOE_FC):
            cs = slice(c * MOE_FC, (c + 1) * MOE_FC)
            gate = jnp.dot(x, wg_ref[:, cs], preferred_element_type=F32)
            up = jnp.dot(x, wu_ref[:, cs], preferred_element_type=F32)
            act = (gate * _sigmoid(gate) * up).astype(BF16)
            acc = acc + jnp.dot(act, wd_ref[cs, :], preferred_element_type=F32)
        _store_row_tiles(ys_ref, (), MOE_TM, acc)


def _moe(tile_expert, tile_valid, tile_src, xs, wg, wu, wd, rows):
    d_exp = wg.shape[2]
    wspec = lambda shape: pl.BlockSpec(shape, lambda i, te, tv, ts: (te[i], 0, 0), pipeline_mode=pl.Buffered(1))
    grid_spec = pltpu.PrefetchScalarGridSpec(
        num_scalar_prefetch=3,
        grid=(rows // MOE_TM,),
        in_specs=[
            pl.BlockSpec((MOE_TM * ROW_TILE, LANES), lambda i, te, tv, ts: (ts[i], 0)),
            wspec((None, D_MODEL, d_exp)), wspec((None, D_MODEL, d_exp)), wspec((None, d_exp, D_MODEL)),
        ],
        out_specs=pl.BlockSpec((MOE_TM * ROW_TILE, LANES), lambda i, te, tv, ts: (i, 0)),
    )
    return pl.pallas_call(
        _moe_kernel,
        grid_spec=grid_spec,
        out_shape=jax.ShapeDtypeStruct((rows * ROW_TILE, LANES), F32),
        compiler_params=_cparams(("arbitrary",)),
        name="experts",
    )(tile_expert, tile_valid, tile_src, xs, wg, wu, wd)


def _combine_kernel(pos_ref, next_ref, x_ref, wc_ref, g_ref, ys_ref, out_ref, buf, sems):
    i = pl.program_id(0)
    n = pl.num_programs(0)
    tm = x_ref.shape[0]

    def copy(slot, idx_ref, r, k):
        return pltpu.make_async_copy(ys_ref.at[_row_tiles(idx_ref[k, r], 1)], buf.at[slot, k, _row_tiles(r, 1)],
                                     sems.at[slot])

    def start_tile(slot, idx_ref):
        def start(r, _):
            copy(slot, idx_ref, r, 0).start(priority=0)
            copy(slot, idx_ref, r, 1).start(priority=1)
            return 0
        lax.fori_loop(0, tm, start, 0, unroll=8)

    slot = i % 2

    @pl.when(i == 0)
    def _():
        start_tile(0, pos_ref)

    @pl.when(i + 1 < n)
    def _():
        start_tile(1 - slot, next_ref)

    def wait(r, _):
        copy(slot, pos_ref, r, 0).wait()
        copy(slot, pos_ref, r, 1).wait()
        return 0

    lax.fori_loop(0, tm, wait, 0, unroll=8)
    wc = wc_ref[...]
    y0 = _load_row_tiles(buf, (slot, 0), tm)
    y1 = _load_row_tiles(buf, (slot, 1), tm)
    y = x_ref[...] + (wc[:, 0:1] * y0 + wc[:, 1:2] * y1)
    out_ref[...] = _rms(y, g_ref[...])


def _combine(pos3, x2, wcols, g, ys):
    tokens = x2.shape[0]
    ntiles = tokens // GATHER_TM
    return pl.pallas_call(
        _combine_kernel,
        grid=(tokens // GATHER_TM,),
        in_specs=[
            pl.BlockSpec((None, TOP_K, GATHER_TM), lambda i: (i, 0, 0), memory_space=pltpu.SMEM),
            pl.BlockSpec((None, TOP_K, GATHER_TM), lambda i: (jnp.minimum(i + 1, ntiles - 1), 0, 0),
                         memory_space=pltpu.SMEM),
            pl.BlockSpec((GATHER_TM, D_MODEL), lambda i: (i, 0)),
            pl.BlockSpec((GATHER_TM, LANES), lambda i: (i, 0)),
            _resident((1, D_MODEL), lambda i: (0, 0)),
            pl.BlockSpec(memory_space=pl.ANY),
        ],
        out_specs=pl.BlockSpec((GATHER_TM, D_MODEL), lambda i: (i, 0)),
        out_shape=jax.ShapeDtypeStruct((tokens, D_MODEL), F32),
        scratch_shapes=[pltpu.VMEM((2, TOP_K, GATHER_TM * ROW_TILE, LANES), F32), pltpu.SemaphoreType.DMA((2,))],
        compiler_params=_cparams(("arbitrary",)),
        name="combine",
    )(pos3, pos3, x2, wcols, g, ys)


def _rotary_lane_tables(seq):
    half = ROT_DIM // 2
    pos = jnp.arange(seq, dtype=F32)
    inv_freq = ROPE_THETA ** (-jnp.arange(0, ROT_DIM, 2, dtype=F32) / ROT_DIM)
    ang = pos[:, None] * inv_freq[None, :]
    cos, sin = jnp.cos(ang), jnp.sin(ang)
    j = np.arange(LANES) % HEAD_DIM
    idx = j % half
    cosf = jnp.where((j < ROT_DIM)[None, :], cos[:, idx], 1.0)
    sina = jnp.where(((j >= half) & (j < ROT_DIM))[None, :], sin[:, idx], 0.0)
    sinb = jnp.where((j < half)[None, :], -sin[:, idx], 0.0)
    return cosf, sina, sinb


def _mixer(x2, layer, p, tables, batch, seq):
    cuts = np.cumsum([D_RNN, D_RNN, ATT_WIDTH, ATT_WIDTH, ATT_WIDTH])
    w = p["w_in"][layer]
    w_perm = jnp.concatenate([w[:, :cuts[1]], w[:, cuts[4]:], w[:, cuts[1]:cuts[4]]], axis=1).astype(BF16)
    xc, yg, qkv4 = _in_proj(x2, p["norm_mix_g"][layer][None], w_perm, p["conv_w"][layer],
                            p["conv_b"][layer][None], *tables, batch, seq)

    wgate = (0.5 * jnp.concatenate([p["w_rgate"][layer], p["w_igate"][layer]], axis=-1)).astype(BF16)
    hf, hb = _lru(xc.reshape(batch, seq, D_RNN), wgate, 0.5 * p["b_rgate"][layer], 0.5 * p["b_igate"][layer],
                  p["lru_lambda"][layer])
    att = _attention(qkv4, batch, seq)

    tokens = batch * seq
    return (x2, hf.reshape(tokens, D_RNN), hb.reshape(tokens, D_RNN), yg, att,
            p["w_proj_lru"][layer].astype(BF16), p["w_proj_att"][layer].astype(BF16), p["w_out"][layer].astype(BF16))


def _moe_layer(merge_args, j, p, final_g):
    tokens = merge_args[0].shape[0]
    wr = jnp.pad(p["w_router"][j], ((0, 0), (0, LANES - N_EXPERTS)))
    whi = wr.astype(BF16)
    wlo = (wr - whi.astype(F32)).astype(BF16)
    x2 = _merge(merge_args)
    hp, wcols, meta, counts = _router(x2, p["norm_ffn_g"][j * 2 + 1][None], whi, wlo)

    counts = counts[0, :N_EXPERTS]
    padded = ((counts + MOE_TM - 1) // MOE_TM) * MOE_TM
    ends = jnp.cumsum(padded)
    starts = ends - padded
    experts_of = meta[0:TOP_K]
    base = jnp.zeros_like(experts_of)
    for e in range(N_EXPERTS):
        base = base + jnp.where(experts_of == e, starts[e], 0)
    pos = base + meta[TOP_K:2 * TOP_K]
    pos3 = pos.reshape(TOP_K, tokens // GATHER_TM, GATHER_TM).transpose(1, 0, 2)

    rows = TOP_K * tokens + N_EXPERTS * MOE_TM
    tile_start = jnp.arange(rows // MOE_TM, dtype=jnp.int32) * MOE_TM
    tile_expert = jnp.minimum(jnp.sum(tile_start[:, None] >= ends[None, :], axis=1), N_EXPERTS - 1).astype(jnp.int32)
    tile_valid = (tile_start < ends[-1]).astype(jnp.int32)
    tile_src = jnp.minimum(tile_start // MOE_TM, ends[-1] // MOE_TM - 1).astype(jnp.int32)
    pad_start8 = ((starts + counts) // SUBLANES * SUBLANES).astype(jnp.int32)
    pad_start8 = jnp.concatenate([pad_start8, ends[-1:].astype(jnp.int32)])

    xs = _dispatch(pad_start8, pos3, hp, rows)
    ys = _moe(tile_expert, tile_valid, tile_src, xs, p["w_exp_gate"][j].astype(BF16),
              p["w_exp_up"][j].astype(BF16), p["w_exp_down"][j].astype(BF16), rows)
    return _combine(pos3, x2, wcols, final_g[None], ys)


def kernel(x, norm_mix_g, w_in, conv_w, conv_b, w_rgate, b_rgate, w_igate, b_igate, lru_lambda, w_proj_lru,
           w_proj_att, w_out, norm_ffn_g, w_dense_gate, w_dense_up, w_dense_down, w_router, w_exp_gate, w_exp_up,
           w_exp_down, final_norm_g):
    batch, seq, _ = x.shape
    depth = w_in.shape[0]
    assert depth % 2 == 0, "the final RMSNorm is fused into the last (routed) layer"
    p = dict(norm_mix_g=norm_mix_g, w_in=w_in, conv_w=conv_w, conv_b=conv_b, w_rgate=w_rgate, b_rgate=b_rgate,
             w_igate=w_igate, b_igate=b_igate, lru_lambda=lru_lambda, w_proj_lru=w_proj_lru,
             w_proj_att=w_proj_att, w_out=w_out, norm_ffn_g=norm_ffn_g, w_router=w_router,
             w_exp_gate=w_exp_gate, w_exp_up=w_exp_up, w_exp_down=w_exp_down)
    tables = _rotary_lane_tables(seq)
    x2 = x.reshape(batch * seq, D_MODEL)
    for layer in range(depth):
        merge_args = _mixer(x2, layer, p, tables, batch, seq)
        j = layer // 2
        if layer % 2 == 0:
            x2 = _merge_ffn(merge_args, norm_ffn_g[layer][None], w_dense_gate[j].astype(BF16),
                            w_dense_up[j].astype(BF16), w_dense_down[j].astype(BF16))
        else:
            assert layer == depth - 1
            x2 = _moe_layer(merge_args, j, p, final_norm_g)
    return x2.reshape(batch, seq, D_MODEL)
```

```python
import functools

import jax
import jax.numpy as jnp
import numpy as np
from jax import lax
from jax.experimental import pallas as pl
from jax.experimental.pallas import tpu as pltpu

D_MODEL = 1024
D_RNN = D_MODEL
N_LRU_BLOCKS = 8
LRU_BLOCK = D_RNN // N_LRU_BLOCKS
CONV_WIDTH = 4
LRU_C = 8.0
HEAD_DIM = 64
HEADS_PER_GROUP = 4
WINDOWS = (128, 512, 2048)
DILATIONS = (1, 4, 16)
N_GROUPS = len(WINDOWS)
ATT_WIDTH = N_GROUPS * HEADS_PER_GROUP * HEAD_DIM
ATT_OUT = HEADS_PER_GROUP * HEAD_DIM
ROT_DIM = HEAD_DIM // 4
ROPE_THETA = 500000.0
HALF_KEYS = 64
IN_WIDTH = 2 * D_RNN + 3 * ATT_WIDTH + 2 * D_MODEL
QKV_WIDTH = 3 * ATT_WIDTH
YG_WIDTH = 3 * D_MODEL
N_EXPERTS = 8
TOP_K = 2
RMS_EPS = 1e-6

LANES = 128
SUBLANES = 8
VMEM_LIMIT = 52 * 1024 * 1024
FUSED_VMEM_LIMIT = 58 * 1024 * 1024

TM = 512
LRU_TS = 256
LRU_CB = 256
ATT_CQ = 128
ATT_KW = ATT_CQ + 2 * HALF_KEYS
ATT_UNROLL_ROWS = 8 * ATT_CQ
MOE_TM = 512
MOE_FC = 512
GATHER_TM = 512
FILL_ROWS = MOE_TM + SUBLANES

BF16 = jnp.bfloat16
F32 = jnp.float32
LOG2E = float(np.log2(np.e))
LN2 = float(np.log(2.0))
Q_SCALE = HEAD_DIM ** -0.5 * LOG2E
SQRT_FLOOR = 1e-30


def _cparams(sem):
    return pltpu.CompilerParams(dimension_semantics=sem, vmem_limit_bytes=VMEM_LIMIT)


def _resident(shape, index_map):
    return pl.BlockSpec(shape, index_map, pipeline_mode=pl.Buffered(1))


def _sigmoid(x):
    return 0.5 * jnp.tanh(0.5 * x) + 0.5


def _rms(x, g):
    ms = jnp.mean(x * x, axis=-1, keepdims=True)
    return x * lax.rsqrt(ms + RMS_EPS) * g


def _in_proj_kernel(seq_tiles, x_ref, xp_ref, xn_ref, g_ref, w_ref, cw_ref, cb_ref, cos_ref, sa_ref, sb_ref,
                    xc_ref, yg_ref, qkv_ref):
    i = pl.program_id(0)
    tm = x_ref.shape[0]
    halo = xp_ref.shape[0]
    xp = jnp.where((i % seq_tiles) == 0, 0.0, xp_ref[...])
    xn = jnp.where((i % seq_tiles) == seq_tiles - 1, 0.0, xn_ref[...])
    hn_ext = _rms(jnp.concatenate([xp, x_ref[...], xn], axis=0), g_ref[...]).astype(BF16)
    hn = hn_ext[halo:halo + tm]

    def proj(c0, width):
        return jnp.dot(hn, w_ref[:, c0:c0 + width], preferred_element_type=F32)

    cw = cw_ref[...]
    for c in range(D_RNN // 512):
        cs = slice(c * 512, (c + 1) * 512)
        ext = jnp.dot(hn_ext, w_ref[:, cs], preferred_element_type=F32)
        xc = cb_ref[:, cs]
        for j in range(CONV_WIDTH):
            xc = xc + ext[halo - 1 + j:halo - 1 + j + tm] * cw[j:j + 1, cs]
        xc_ref[:, cs] = xc
    for c in range(YG_WIDTH // 512):
        yg_ref[:, c * 512:(c + 1) * 512] = proj(D_RNN + c * 512, 512).astype(BF16)

    cosf, sina, sinb = cos_ref[...], sa_ref[...], sb_ref[...]
    base = D_RNN + YG_WIDTH
    for c in range(QKV_WIDTH // 256):
        acc = proj(base + c * 256, 256)
        if c < 2 * ATT_WIDTH // 256:
            for h in range(2):
                t = acc[:, h * LANES:(h + 1) * LANES]
                r = t * cosf + pltpu.roll(t, 8, 1) * sina + pltpu.roll(t, LANES - 8, 1) * sinb
                qkv_ref[2 * c + h] = r * Q_SCALE if c < ATT_WIDTH // 256 else r
        else:
            for h in range(2):
                qkv_ref[2 * c + h] = acc[:, h * LANES:(h + 1) * LANES]


def _in_proj(x2, g, w_perm, conv_w, conv_b, cosf, sina, sinb, batch, seq):
    tokens = batch * seq
    ns = seq // TM
    halo_blocks = TM // SUBLANES
    return pl.pallas_call(
        functools.partial(_in_proj_kernel, ns),
        grid=(tokens // TM,),
        in_specs=[
            pl.BlockSpec((TM, D_MODEL), lambda i: (i, 0)),
            pl.BlockSpec((SUBLANES, D_MODEL), lambda i: (jnp.maximum(i * halo_blocks - 1, 0), 0)),
            pl.BlockSpec((SUBLANES, D_MODEL),
                         lambda i: (jnp.minimum((i + 1) * halo_blocks, tokens // SUBLANES - 1), 0)),
            _resident((1, D_MODEL), lambda i: (0, 0)),
            _resident((D_MODEL, IN_WIDTH), lambda i: (0, 0)),
            _resident((CONV_WIDTH, D_RNN), lambda i: (0, 0)),
            _resident((1, D_RNN), lambda i: (0, 0)),
            pl.BlockSpec((TM, LANES), lambda i: (i % ns, 0)),
            pl.BlockSpec((TM, LANES), lambda i: (i % ns, 0)),
            pl.BlockSpec((TM, LANES), lambda i: (i % ns, 0)),
        ],
        out_specs=[
            pl.BlockSpec((TM, D_RNN), lambda i: (i, 0)),
            pl.BlockSpec((TM, YG_WIDTH), lambda i: (i, 0)),
            pl.BlockSpec((None, QKV_WIDTH // LANES, TM, LANES), lambda i: (i // ns, 0, i % ns, 0)),
        ],
        out_shape=[
            jax.ShapeDtypeStruct((tokens, D_RNN), F32),
            jax.ShapeDtypeStruct((tokens, YG_WIDTH), BF16),
            jax.ShapeDtypeStruct((batch, QKV_WIDTH // LANES, seq, LANES), F32),
        ],
        compiler_params=_cparams(("parallel",)),
        name="in_proj",
    )(x2, x2, x2, g, w_perm, conv_w, conv_b, cosf, sina, sinb)


def _lru_kernel(x_hbm, wg_ref, br_ref, bi_ref, lam_ref, hf_ref, hb_ref, xbuf, a_scr, u_scr, h_scr, carry_scr, sems):
    n, c = pl.program_id(0), pl.program_id(1)
    nc = pl.num_programs(1)
    step_id = n * nc + c
    last_step = pl.num_programs(0) * nc - 1
    nb, ts = hf_ref.shape[0], hf_ref.shape[1]

    @pl.when(c == 0)
    def _():
        carry_scr[...] = jnp.zeros_like(carry_scr)

    def copies(sn, sc, slot):
        col = pl.ds(pl.multiple_of(sn * LRU_CB, LRU_CB), LRU_CB)
        out = []
        for direction in range(2):
            chunk = sc if direction == 0 else nc - 1 - sc
            rows = pl.ds(pl.multiple_of(chunk * ts, ts), ts)
            out += [pltpu.make_async_copy(x_hbm.at[b, rows, col], xbuf.at[slot, direction, :, b, :],
                                          sems.at[slot, direction]) for b in range(nb)]
        return out

    slot = step_id % 2

    @pl.when(step_id == 0)
    def _():
        for cp in copies(n, c, 0):
            cp.start()

    @pl.when(step_id < last_step)
    def _():
        nxt = step_id + 1
        for cp in copies(nxt // nc, nxt % nc, 1 - slot):
            cp.start()

    for cp in copies(n, c, slot):
        cp.wait()

    def gates(direction):
        xc = xbuf[slot, direction]
        for j in range(LRU_CB // LRU_BLOCK):
            sl = slice(j * LRU_BLOCK, (j + 1) * LRU_BLOCK)
            xcb = xc[:, :, sl].reshape(ts * nb, LRU_BLOCK)
            rg = jnp.dot(xcb.astype(BF16), wg_ref[direction, j], preferred_element_type=F32)
            tr = jnp.tanh(rg[:, :LRU_BLOCK] + br_ref[direction:direction + 1, sl])
            ti = jnp.tanh(rg[:, LRU_BLOCK:] + bi_ref[direction:direction + 1, sl])
            lam = lam_ref[direction:direction + 1, sl]
            c2 = (-0.5 * LRU_C * LOG2E) * (jnp.maximum(-lam, 0.0) + jnp.log(1.0 + jnp.exp(-jnp.abs(lam))))
            a = jnp.exp2(tr * c2 + c2)
            xh = 0.5 * xcb
            gated = ti * xh + xh
            om = 1.0 - a * a
            u = (om * lax.rsqrt(jnp.maximum(om, SQRT_FLOOR))) * gated
            a_scr[direction, :, :, sl] = a.reshape(ts, nb, LRU_BLOCK)
            u_scr[direction, :, :, sl] = u.reshape(ts, nb, LRU_BLOCK)

    gates(0)
    gates(1)

    def step(t, carry):
        hf, hb = carry
        tb = ts - 1 - t
        hf = a_scr[0, t] * hf + u_scr[0, t]
        hb = a_scr[1, tb] * hb + u_scr[1, tb]
        h_scr[0, t] = hf
        h_scr[1, tb] = hb
        return hf, hb

    hf, hb = lax.fori_loop(0, ts, step, (carry_scr[0], carry_scr[1]), unroll=8)
    carry_scr[0] = hf
    carry_scr[1] = hb
    hf_ref[...] = jnp.swapaxes(h_scr[0], 0, 1).astype(hf_ref.dtype)
    hb_ref[...] = jnp.swapaxes(h_scr[1], 0, 1).astype(hb_ref.dtype)


def _lru(xc3, wgate, b_r, b_i, lam):
    batch, seq, _ = xc3.shape
    nc = seq // LRU_TS
    ncb = D_RNN // LRU_CB

    def main(rev):
        return pl.BlockSpec((batch, LRU_TS, LRU_CB), lambda n, c: (0, (nc - 1 - c) if rev else c, n))

    vec2 = pl.BlockSpec((2, LRU_CB), lambda n, c: (0, n))
    out_sds = jax.ShapeDtypeStruct((batch, seq, D_RNN), BF16)
    return pl.pallas_call(
        _lru_kernel,
        grid=(ncb, nc),
        in_specs=[
            pl.BlockSpec(memory_space=pl.ANY),
            pl.BlockSpec((2, LRU_CB // LRU_BLOCK, LRU_BLOCK, 2 * LRU_BLOCK), lambda n, c: (0, n, 0, 0)),
            vec2, vec2, vec2,
        ],
        out_specs=[main(False), main(True)],
        out_shape=[out_sds, out_sds],
        scratch_shapes=[
            pltpu.VMEM((2, 2, LRU_TS, batch, LRU_CB), F32),
            pltpu.VMEM((2, LRU_TS, batch, LRU_CB), F32),
            pltpu.VMEM((2, LRU_TS, batch, LRU_CB), F32),
            pltpu.VMEM((2, LRU_TS, batch, LRU_CB), F32),
            pltpu.VMEM((2, batch, LRU_CB), F32),
            pltpu.SemaphoreType.DMA((2, 2)),
        ],
        compiler_params=_cparams(("arbitrary", "arbitrary")),
        name="lru",
    )(xc3, wgate, b_r, b_i, lam)


def _attn_kernel(q_ref, k_ref, v_ref, bias_ref, wide_bias_ref, out_ref, acc_o, acc_l):
    g = pl.program_id(1)
    seq = q_ref.shape[1]
    nblk = ATT_OUT // LANES
    lane_head = lax.broadcasted_iota(jnp.int32, (1, ATT_OUT), 1) // HEAD_DIM

    def rows(start, n, d):
        return pl.ds(start, n) if d == 1 else pl.ds(start, n, stride=d)

    def load(ref, start, n, d):
        return jnp.concatenate([ref[j, rows(start, n, d), :] for j in range(nblk)], axis=1)

    def store(ref, start, n, d, val):
        for j in range(nblk):
            ref[j, rows(start, n, d), :] = val[:, j * LANES:(j + 1) * LANES]

    def run_group(gi):
        d = DILATIONS[gi]
        length = seq // d
        cq = length if length == ATT_KW else ATT_CQ
        chunks_per_residue = length // cq

        def chunk(n, _):
            r = n // chunks_per_residue
            qs = (n - r * chunks_per_residue) * cq
            ks = jnp.clip(qs - HALF_KEYS, 0, length - ATT_KW)
            q0, k0 = r + d * qs, r + d * ks
            if d == 1:
                q0, k0 = pl.multiple_of(q0, ATT_CQ), pl.multiple_of(k0, HALF_KEYS)
            q = load(q_ref, q0, cq, d).astype(BF16)
            kk = load(k_ref, k0, ATT_KW, d).astype(BF16)
            vv = load(v_ref, k0, ATT_KW, d).astype(BF16)
            q4 = jnp.concatenate([jnp.where(lane_head == h, q, jnp.zeros_like(q))
                                  for h in range(HEADS_PER_GROUP)], axis=0)
            s = lax.dot_general(q4, kk, (((1,), (1,)), ((), ())), preferred_element_type=F32)
            s = s + (bias_ref[(qs - ks) // HALF_KEYS] if cq == ATT_CQ else wide_bias_ref[...])
            m = jnp.max(s, axis=1, keepdims=True)
            p = jnp.exp2(s - m)
            l = jnp.sum(p, axis=1, keepdims=True)
            pv = jnp.dot(p.astype(BF16), vv, preferred_element_type=F32)
            inv = 1.0 / l
            lse = m * LN2 + jnp.log(l)
            o = jnp.zeros((cq, ATT_OUT), F32)
            lf = jnp.zeros((cq, ATT_OUT), F32)
            for h in range(HEADS_PER_GROUP):
                sl = slice(h * cq, (h + 1) * cq)
                sel = lane_head == h
                o = jnp.where(sel, pv[sl] * inv[sl], o)
                lf = jnp.where(sel, lse[sl], lf)
            if gi == 0:
                store(acc_o, q0, cq, d, o)
                store(acc_l, q0, cq, d, lf)
            else:
                ao = load(acc_o, q0, cq, d)
                al = load(acc_l, q0, cq, d)
                top = jnp.maximum(al, lf)
                wa, wg = jnp.exp(al - top), jnp.exp(lf - top)
                den = wa + wg
                store(acc_o, q0, cq, d, (wa * ao + wg * o) * (1.0 / den))
                if gi < N_GROUPS - 1:
                    store(acc_l, q0, cq, d, top + jnp.log(den))
            return 0

        lax.fori_loop(0, seq // cq, chunk, 0, unroll=ATT_UNROLL_ROWS // cq)

    for gi in range(N_GROUPS):
        pl.when(g == gi)(functools.partial(run_group, gi))

    @pl.when(g == N_GROUPS - 1)
    def _():
        out_ref[...] = jnp.concatenate([acc_o[j] for j in range(nblk)], axis=1).astype(out_ref.dtype)


def _attention(qkv4, batch, seq):
    per_kind = ATT_WIDTH // ATT_OUT
    nblk = ATT_OUT // LANES

    def spec(kind):
        return pl.BlockSpec((None, nblk, seq, LANES), lambda b, g: (b, kind * per_kind + g, 0, 0))

    rows4 = HEADS_PER_GROUP * ATT_CQ
    qrow = (np.arange(rows4) % ATT_CQ)[None, :, None]
    kcol = np.arange(ATT_KW)[None, None, :]
    shift = (np.arange(3) * HALF_KEYS)[:, None, None]
    bias = jnp.asarray(np.where(np.abs(kcol - shift - qrow) <= HALF_KEYS, 0.0, -np.inf), F32)
    wide_rows = HEADS_PER_GROUP * ATT_KW
    wide_q = (np.arange(wide_rows) % ATT_KW)[:, None]
    wide_bias = jnp.asarray(np.where(np.abs(np.arange(ATT_KW)[None, :] - wide_q) <= HALF_KEYS, 0.0, -np.inf), F32)

    return pl.pallas_call(
        _attn_kernel,
        grid=(batch, N_GROUPS),
        in_specs=[spec(0), spec(1), spec(2), _resident((3, rows4, ATT_KW), lambda b, g: (0, 0, 0)),
                  _resident((wide_rows, ATT_KW), lambda b, g: (0, 0))],
        out_specs=pl.BlockSpec((seq, ATT_OUT), lambda b, g: (b, 0)),
        out_shape=jax.ShapeDtypeStruct((batch * seq, ATT_OUT), BF16),
        scratch_shapes=[pltpu.VMEM((nblk, seq, LANES), F32), pltpu.VMEM((nblk, seq, LANES), F32)],
        compiler_params=_cparams(("parallel", "arbitrary")),
        name="attention",
    )(qkv4, qkv4, qkv4, bias, wide_bias)


def _merge_body(x_ref, hf_ref, hb_ref, yg_ref, att_ref, wpl_ref, wpa_ref, wo_ref):
    y = yg_ref[:, :D_MODEL].astype(F32)
    gelu = 0.5 * y * (1.0 + jnp.tanh(np.sqrt(2.0 / np.pi) * (y + 0.044715 * (y * y * y))))
    lru = ((hf_ref[...].astype(F32) + hb_ref[...].astype(F32)) * gelu).astype(BF16)
    branch_lru = jnp.dot(lru, wpl_ref[...], preferred_element_type=F32)
    branch_att = jnp.dot(att_ref[...], wpa_ref[...], preferred_element_type=F32)

    g_lru = _sigmoid(yg_ref[:, D_MODEL:2 * D_MODEL].astype(F32))
    g_att = _sigmoid(yg_ref[:, 2 * D_MODEL:].astype(F32))
    merged = (g_lru * branch_lru + g_att * branch_att).astype(BF16)
    return x_ref[...] + jnp.dot(merged, wo_ref[...], preferred_element_type=F32)


def _merge_specs():
    tok = lambda w: pl.BlockSpec((TM, w), lambda i: (i, 0))
    return [tok(D_MODEL), tok(D_RNN), tok(D_RNN), tok(YG_WIDTH), tok(ATT_OUT),
            _resident((D_RNN, D_MODEL), lambda i: (0, 0)),
            _resident((ATT_OUT, D_MODEL), lambda i: (0, 0)),
            _resident((D_MODEL, D_MODEL), lambda i: (0, 0))]


def _merge_kernel(*refs):
    refs[8][...] = _merge_body(*refs[:8])


def _merge(merge_args):
    tokens = merge_args[0].shape[0]
    return pl.pallas_call(
        _merge_kernel,
        grid=(tokens // TM,),
        in_specs=_merge_specs(),
        out_specs=pl.BlockSpec((TM, D_MODEL), lambda i: (i, 0)),
        out_shape=jax.ShapeDtypeStruct((tokens, D_MODEL), F32),
        compiler_params=_cparams(("parallel",)),
        name="merge",
    )(*merge_args)


def _ffn_chunks(d_ff):
    tiles = d_ff // 256
    sizes = [tiles // 3 + (1 if k < tiles % 3 else 0) for k in range(3)]
    out, start = [], 0
    for s in sizes:
        if s:
            out.append((start * 256, s * 256))
            start += s
    return out


def _merge_ffn_kernel(*refs):
    merge_refs, (g_ref, wg_ref, wu_ref, wd_ref, out_ref) = refs[:8], refs[8:]
    x = _merge_body(*merge_refs)
    hn = _rms(x, g_ref[...]).astype(BF16)
    acc = x
    for c0, width in _ffn_chunks(wg_ref.shape[1]):
        gate = jnp.dot(hn, wg_ref[:, c0:c0 + width], preferred_element_type=F32)
        up = jnp.dot(hn, wu_ref[:, c0:c0 + width], preferred_element_type=F32)
        act = (gate * _sigmoid(gate) * up).astype(BF16)
        acc = acc + jnp.dot(act, wd_ref[c0:c0 + width, :], preferred_element_type=F32)
    out_ref[...] = acc


def _merge_ffn(merge_args, g, wg, wu, wd):
    tokens = merge_args[0].shape[0]
    d_ff = wg.shape[1]
    return pl.pallas_call(
        _merge_ffn_kernel,
        grid=(tokens // TM,),
        in_specs=_merge_specs() + [
            _resident((1, D_MODEL), lambda i: (0, 0)),
            _resident((D_MODEL, d_ff), lambda i: (0, 0)),
            _resident((D_MODEL, d_ff), lambda i: (0, 0)),
            _resident((d_ff, D_MODEL), lambda i: (0, 0)),
        ],
        out_specs=pl.BlockSpec((TM, D_MODEL), lambda i: (i, 0)),
        out_shape=jax.ShapeDtypeStruct((tokens, D_MODEL), F32),
        compiler_params=pltpu.CompilerParams(dimension_semantics=("parallel",), vmem_limit_bytes=FUSED_VMEM_LIMIT),
        name="merge_ffn",
    )(*merge_args, g, wg, wu, wd)


ROW_TILE = D_MODEL // LANES
assert ROW_TILE == SUBLANES


def _store_row_tiles(ref, index, rows, val):
    for s in range(ROW_TILE):
        ref[(*index, pl.ds(s, rows, stride=ROW_TILE), slice(None))] = val[:, s * LANES:(s + 1) * LANES]


def _row_tiles(row, n):
    start = row * ROW_TILE
    return pl.ds(start if isinstance(row, int) else pl.multiple_of(start, ROW_TILE), n * ROW_TILE)


def _load_row_tiles(ref, index, rows):
    return jnp.concatenate([ref[(*index, pl.ds(s, rows, stride=ROW_TILE), slice(None))]
                            for s in range(ROW_TILE)], axis=1)


def _router_kernel(x_ref, g_ref, whi_ref, wlo_ref, tri_ref, hp_ref, wc_ref, meta_ref, cnt_ref, carry_scr):
    i = pl.program_id(0)

    @pl.when(i == 0)
    def _():
        carry_scr[...] = jnp.zeros_like(carry_scr)

    hn = _rms(x_ref[...], g_ref[...])
    hi = hn.astype(BF16)
    lo = (hn - hi.astype(F32)).astype(BF16)
    _store_row_tiles(hp_ref, (), hn.shape[0], hn)

    whi = whi_ref[...]
    logits = (jnp.dot(hi, whi, preferred_element_type=F32) + jnp.dot(lo, whi, preferred_element_type=F32)
              + jnp.dot(hi, wlo_ref[...], preferred_element_type=F32))
    tm = logits.shape[0]
    lane = lax.broadcasted_iota(jnp.int32, (tm, LANES), 1).astype(F32)
    lg = jnp.where(lane < N_EXPERTS, logits, -jnp.inf)
    m1 = jnp.max(lg, axis=1, keepdims=True)
    i1 = jnp.min(jnp.where(lg == m1, lane, float(LANES)), axis=1, keepdims=True)
    lg2 = jnp.where(lane == i1, -jnp.inf, lg)
    m2 = jnp.max(lg2, axis=1, keepdims=True)
    i2 = jnp.min(jnp.where(lg2 == m2, lane, float(LANES)), axis=1, keepdims=True)
    e = jnp.exp(m2 - m1)
    w1 = 1.0 / (1.0 + e)
    w2 = e * w1
    wc_ref[...] = jnp.where(lane == 0.0, w1, jnp.where(lane == 1.0, w2, 0.0))

    hot1 = lane == i1
    hot2 = lane == i2
    onehot = jnp.where(hot1 | hot2, 1.0, 0.0).astype(BF16)
    cum = jnp.dot(tri_ref[...], onehot, preferred_element_type=F32)
    tot = carry_scr[...] + cum
    r1 = jnp.sum(jnp.where(hot1, tot - 1.0, 0.0), axis=1, keepdims=True)
    r2 = jnp.sum(jnp.where(hot2, tot - 1.0, 0.0), axis=1, keepdims=True)
    carry_scr[...] = tot[tm - 1:tm, :]
    cnt_ref[...] = tot[tm - 1:tm, :].astype(jnp.int32)

    cols = jnp.where(lane == 0.0, i1, jnp.where(lane == 1.0, i2, jnp.where(lane == 2.0, r1,
                     jnp.where(lane == 3.0, r2, 0.0))))
    meta_ref[...] = cols.T[:SUBLANES, :].astype(jnp.int32)


def _router(x2, g, whi, wlo):
    tokens = x2.shape[0]
    tri = jnp.tril(jnp.ones((TM, TM), F32)).astype(BF16)
    return pl.pallas_call(
        _router_kernel,
        grid=(tokens // TM,),
        in_specs=[
            pl.BlockSpec((TM, D_MODEL), lambda i: (i, 0)),
            _resident((1, D_MODEL), lambda i: (0, 0)),
            _resident((D_MODEL, LANES), lambda i: (0, 0)),
            _resident((D_MODEL, LANES), lambda i: (0, 0)),
            _resident((TM, TM), lambda i: (0, 0)),
        ],
        out_specs=[
            pl.BlockSpec((TM * ROW_TILE, LANES), lambda i: (i, 0)),
            pl.BlockSpec((TM, LANES), lambda i: (i, 0)),
            pl.BlockSpec((SUBLANES, TM), lambda i: (0, i)),
            pl.BlockSpec((1, LANES), lambda i: (0, 0)),
        ],
        out_shape=[
            jax.ShapeDtypeStruct((tokens * ROW_TILE, LANES), F32),
            jax.ShapeDtypeStruct((tokens, LANES), F32),
            jax.ShapeDtypeStruct((SUBLANES, tokens), jnp.int32),
            jax.ShapeDtypeStruct((1, LANES), jnp.int32),
        ],
        scratch_shapes=[pltpu.VMEM((1, LANES), F32)],
        compiler_params=_cparams(("arbitrary",)),
        name="router",
    )(x2, g, whi, wlo, tri)


def _dispatch_kernel(pad_ref, pos_ref, prev_ref, hp_ref, xs_ref, stage, zeros, sems, zsem):
    i = pl.program_id(0)
    n = pl.num_programs(0)
    tm = pos_ref.shape[1]

    @pl.when(i == 0)
    def _():
        zeros[...] = jnp.zeros_like(zeros)
        fill_rows = zeros.shape[0] // ROW_TILE
        used = pad_ref[N_EXPERTS]
        total = xs_ref.shape[0] // ROW_TILE - fill_rows

        def fill(start):
            return pltpu.make_async_copy(zeros, xs_ref.at[_row_tiles(start, fill_rows)], zsem)

        def tail(t):
            return pltpu.make_async_copy(zeros.at[_row_tiles(0, MOE_TM)], xs_ref.at[_row_tiles(t * MOE_TM, MOE_TM)],
                                         zsem)

        def run(cp):
            cp.start()
            cp.wait()

        for e in range(N_EXPERTS):
            run(fill(pad_ref[e]))
        run(fill(total))
        for t in range(total // MOE_TM - N_EXPERTS, total // MOE_TM):
            pl.when(t * MOE_TM >= used)(functools.partial(run, tail(t)))

    def copy(tile, idx_ref, r, k):
        slot = tile % 2
        return pltpu.make_async_copy(stage.at[slot, _row_tiles(r, 1)], xs_ref.at[_row_tiles(idx_ref[k, r], 1)],
                                     sems.at[slot])

    stage[i % 2] = hp_ref[...]

    def start(r, _):
        copy(i, pos_ref, r, 0).start(priority=0)
        copy(i, pos_ref, r, 1).start(priority=1)
        return 0

    def wait_tile(tile, idx_ref):
        def wait(r, _):
            copy(tile, idx_ref, r, 0).wait()
            copy(tile, idx_ref, r, 1).wait()
            return 0
        lax.fori_loop(0, tm, wait, 0, unroll=8)

    lax.fori_loop(0, tm, start, 0, unroll=8)

    @pl.when(i > 0)
    def _():
        wait_tile(i - 1, prev_ref)

    @pl.when(i == n - 1)
    def _():
        wait_tile(i, pos_ref)


def _dispatch(pad_start8, pos3, hp, rows):
    tokens = hp.shape[0] // ROW_TILE
    return pl.pallas_call(
        _dispatch_kernel,
        grid=(tokens // GATHER_TM,),
        in_specs=[
            pl.BlockSpec(memory_space=pltpu.SMEM),
            pl.BlockSpec((None, TOP_K, GATHER_TM), lambda i: (i, 0, 0), memory_space=pltpu.SMEM),
            pl.BlockSpec((None, TOP_K, GATHER_TM), lambda i: (jnp.maximum(i - 1, 0), 0, 0),
                         memory_space=pltpu.SMEM),
            pl.BlockSpec((GATHER_TM * ROW_TILE, LANES), lambda i: (i, 0)),
        ],
        out_specs=pl.BlockSpec(memory_space=pl.ANY),
        out_shape=jax.ShapeDtypeStruct(((rows + FILL_ROWS) * ROW_TILE, LANES), F32),
        scratch_shapes=[pltpu.VMEM((2, GATHER_TM * ROW_TILE, LANES), F32),
                        pltpu.VMEM((FILL_ROWS * ROW_TILE, LANES), F32),
                        pltpu.SemaphoreType.DMA((2,)), pltpu.SemaphoreType.DMA],
        compiler_params=_cparams(("arbitrary",)),
        name="dispatch",
    )(pad_start8, pos3, pos3, hp)


def _moe_kernel(te_ref, tv_ref, ts_ref, xs_ref, wg_ref, wu_ref, wd_ref, ys_ref):
    del ts_ref
    i = pl.program_id(0)

    @pl.when(tv_ref[i] == 0)
    def _():
        ys_ref[...] = jnp.zeros_like(ys_ref)

    @pl.when(tv_ref[i] != 0)
    def _():
        x = _load_row_tiles(xs_ref, (), MOE_TM).astype(BF16)
        acc = jnp.zeros((MOE_TM, D_MODEL), F32)
        for c in range(wg_ref.shape[1] // MOE_FC):
            cs = slice(c * MOE_FC, (c + 1) * MOE_FC)
            gate = jnp.dot(x, wg_ref[:, cs], preferred_element_type=F32)
            up = jnp.dot(x, wu_ref[:, cs], preferred_element_type=F32)
            act = (gate * _sigmoid(gate) * up).astype(BF16)
            acc = acc + jnp.dot(act, wd_ref[cs, :], preferred_element_type=F32)
        _store_row_tiles(ys_ref, (), MOE_TM, acc)


def _moe(tile_expert, tile_valid, tile_src, xs, wg, wu, wd, rows):
    d_exp = wg.shape[2]
    wspec = lambda shape: pl.BlockSpec(shape, lambda i, te, tv, ts: (te[i], 0, 0), pipeline_mode=pl.Buffered(1))
    grid_spec = pltpu.PrefetchScalarGridSpec(
        num_scalar_prefetch=3,
        grid=(rows // MOE_TM,),
        in_specs=[
            pl.BlockSpec((MOE_TM * ROW_TILE, LANES), lambda i, te, tv, ts: (ts[i], 0)),
            wspec((None, D_MODEL, d_exp)), wspec((None, D_MODEL, d_exp)), wspec((None, d_exp, D_MODEL)),
        ],
        out_specs=pl.BlockSpec((MOE_TM * ROW_TILE, LANES), lambda i, te, tv, ts: (i, 0)),
    )
    return pl.pallas_call(
        _moe_kernel,
        grid_spec=grid_spec,
        out_shape=jax.ShapeDtypeStruct((rows * ROW_TILE, LANES), F32),
        compiler_params=_cparams(("arbitrary",)),
        name="experts",
    )(tile_expert, tile_valid, tile_src, xs, wg, wu, wd)


def _combine_kernel(pos_ref, next_ref, x_ref, wc_ref, g_ref, ys_ref, out_ref, buf, sems):
    i = pl.program_id(0)
    n = pl.num_programs(0)
    tm = x_ref.shape[0]

    def copy(slot, idx_ref, r, k):
        return pltpu.make_async_copy(ys_ref.at[_row_tiles(idx_ref[k, r], 1)], buf.at[slot, k, _row_tiles(r, 1)],
                                     sems.at[slot])

    def start_tile(slot, idx_ref):
        def start(r, _):
            copy(slot, idx_ref, r, 0).start(priority=0)
            copy(slot, idx_ref, r, 1).start(priority=1)
            return 0
        lax.fori_loop(0, tm, start, 0, unroll=8)

    slot = i % 2

    @pl.when(i == 0)
    def _():
        start_tile(0, pos_ref)

    @pl.when(i + 1 < n)
    def _():
        start_tile(1 - slot, next_ref)

    def wait(r, _):
        copy(slot, pos_ref, r, 0).wait()
        copy(slot, pos_ref, r, 1).wait()
        return 0

    lax.fori_loop(0, tm, wait, 0, unroll=8)
    wc = wc_ref[...]
    y0 = _load_row_tiles(buf, (slot, 0), tm)
    y1 = _load_row_tiles(buf, (slot, 1), tm)
    y = x_ref[...] + (wc[:, 0:1] * y0 + wc[:, 1:2] * y1)
    out_ref[...] = _rms(y, g_ref[...])


def _combine(pos3, x2, wcols, g, ys):
    tokens = x2.shape[0]
    ntiles = tokens // GATHER_TM
    return pl.pallas_call(
        _combine_kernel,
        grid=(tokens // GATHER_TM,),
        in_specs=[
            pl.BlockSpec((None, TOP_K, GATHER_TM), lambda i: (i, 0, 0), memory_space=pltpu.SMEM),
            pl.BlockSpec((None, TOP_K, GATHER_TM), lambda i: (jnp.minimum(i + 1, ntiles - 1), 0, 0),
                         memory_space=pltpu.SMEM),
            pl.BlockSpec((GATHER_TM, D_MODEL), lambda i: (i, 0)),
            pl.BlockSpec((GATHER_TM, LANES), lambda i: (i, 0)),
            _resident((1, D_MODEL), lambda i: (0, 0)),
            pl.BlockSpec(memory_space=pl.ANY),
        ],
        out_specs=pl.BlockSpec((GATHER_TM, D_MODEL), lambda i: (i, 0)),
        out_shape=jax.ShapeDtypeStruct((tokens, D_MODEL), F32),
        scratch_shapes=[pltpu.VMEM((2, TOP_K, GATHER_TM * ROW_TILE, LANES), F32), pltpu.SemaphoreType.DMA((2,))],
        compiler_params=_cparams(("arbitrary",)),
        name="combine",
    )(pos3, pos3, x2, wcols, g, ys)


def _rotary_lane_tables(seq):
    half = ROT_DIM // 2
    pos = jnp.arange(seq, dtype=F32)
    inv_freq = ROPE_THETA ** (-jnp.arange(0, ROT_DIM, 2, dtype=F32) / ROT_DIM)
    ang = pos[:, None] * inv_freq[None, :]
    cos, sin = jnp.cos(ang), jnp.sin(ang)
    j = np.arange(LANES) % HEAD_DIM
    idx = j % half
    cosf = jnp.where((j < ROT_DIM)[None, :], cos[:, idx], 1.0)
    sina = jnp.where(((j >= half) & (j < ROT_DIM))[None, :], sin[:, idx], 0.0)
    sinb = jnp.where((j < half)[None, :], -sin[:, idx], 0.0)
    return cosf, sina, sinb


def _mixer(x2, layer, p, tables, batch, seq):
    cuts = np.cumsum([D_RNN, D_RNN, ATT_WIDTH, ATT_WIDTH, ATT_WIDTH])
    w = p["w_in"][layer]
    w_perm = jnp.concatenate([w[:, :cuts[1]], w[:, cuts[4]:], w[:, cuts[1]:cuts[4]]], axis=1).astype(BF16)
    xc, yg, qkv4 = _in_proj(x2, p["norm_mix_g"][layer][None], w_perm, p["conv_w"][layer],
                            p["conv_b"][layer][None], *tables, batch, seq)

    wgate = (0.5 * jnp.concatenate([p["w_rgate"][layer], p["w_igate"][layer]], axis=-1)).astype(BF16)
    hf, hb = _lru(xc.reshape(batch, seq, D_RNN), wgate, 0.5 * p["b_rgate"][layer], 0.5 * p["b_igate"][layer],
                  p["lru_lambda"][layer])
    att = _attention(qkv4, batch, seq)

    tokens = batch * seq
    return (x2, hf.reshape(tokens, D_RNN), hb.reshape(tokens, D_RNN), yg, att,
            p["w_proj_lru"][layer].astype(BF16), p["w_proj_att"][layer].astype(BF16), p["w_out"][layer].astype(BF16))


def _moe_layer(merge_args, j, p, final_g):
    tokens = merge_args[0].shape[0]
    wr = jnp.pad(p["w_router"][j], ((0, 0), (0, LANES - N_EXPERTS)))
    whi = wr.astype(BF16)
    wlo = (wr - whi.astype(F32)).astype(BF16)
    x2 = _merge(merge_args)
    hp, wcols, meta, counts = _router(x2, p["norm_ffn_g"][j * 2 + 1][None], whi, wlo)

    counts = counts[0, :N_EXPERTS]
    padded = ((counts + MOE_TM - 1) // MOE_TM) * MOE_TM
    ends = jnp.cumsum(padded)
    starts = ends - padded
    experts_of = meta[0:TOP_K]
    base = jnp.zeros_like(experts_of)
    for e in range(N_EXPERTS):
        base = base + jnp.where(experts_of == e, starts[e], 0)
    pos = base + meta[TOP_K:2 * TOP_K]
    pos3 = pos.reshape(TOP_K, tokens // GATHER_TM, GATHER_TM).transpose(1, 0, 2)

    rows = TOP_K * tokens + N_EXPERTS * MOE_TM
    tile_start = jnp.arange(rows // MOE_TM, dtype=jnp.int32) * MOE_TM
    tile_expert = jnp.minimum(jnp.sum(tile_start[:, None] >= ends[None, :], axis=1), N_EXPERTS - 1).astype(jnp.int32)
    tile_valid = (tile_start < ends[-1]).astype(jnp.int32)
    tile_src = jnp.minimum(tile_start // MOE_TM, ends[-1] // MOE_TM - 1).astype(jnp.int32)
    pad_start8 = ((starts + counts) // SUBLANES * SUBLANES).astype(jnp.int32)
    pad_start8 = jnp.concatenate([pad_start8, ends[-1:].astype(jnp.int32)])

    xs = _dispatch(pad_start8, pos3, hp, rows)
    ys = _moe(tile_expert, tile_valid, tile_src, xs, p["w_exp_gate"][j].astype(BF16),
              p["w_exp_up"][j].astype(BF16), p["w_exp_down"][j].astype(BF16), rows)
    return _combine(pos3, x2, wcols, final_g[None], ys)


def kernel(x, norm_mix_g, w_in, conv_w, conv_b, w_rgate, b_rgate, w_igate, b_igate, lru_lambda, w_proj_lru,
           w_proj_att, w_out, norm_ffn_g, w_dense_gate, w_dense_up, w_dense_down, w_router, w_exp_gate, w_exp_up,
           w_exp_down, final_norm_g):
    batch, seq, _ = x.shape
    depth = w_in.shape[0]
    assert depth % 2 == 0, "the final RMSNorm is fused into the last (routed) layer"
    p = dict(norm_mix_g=norm_mix_g, w_in=w_in, conv_w=conv_w, conv_b=conv_b, w_rgate=w_rgate, b_rgate=b_rgate,
             w_igate=w_igate, b_igate=b_igate, lru_lambda=lru_lambda, w_proj_lru=w_proj_lru,
             w_proj_att=w_proj_att, w_out=w_out, norm_ffn_g=norm_ffn_g, w_router=w_router,
             w_exp_gate=w_exp_gate, w_exp_up=w_exp_up, w_exp_down=w_exp_down)
    tables = _rotary_lane_tables(seq)
    x2 = x.reshape(batch * seq, D_MODEL)
    for layer in range(depth):
        merge_args = _mixer(x2, layer, p, tables, batch, seq)
        j = layer // 2
        if layer % 2 == 0:
            x2 = _merge_ffn(merge_args, norm_ffn_g[layer][None], w_dense_gate[j].astype(BF16),
                            w_dense_up[j].astype(BF16), w_dense_down[j].astype(BF16))
        else:
            assert layer == depth - 1
            x2 = _moe_layer(merge_args, j, p, final_norm_g)
    return x2.reshape(batch, seq, D_MODEL)
```
